```python
import math
import jax, jax.numpy as jnp
from jax import lax
import numpy as np

D_MODEL = 1024
BATCH = 8
SEQ = 2048
DEPTH = 2

GRID_W = 64
NA_HEADS = 16
NA_HEAD_DIM = 64
ATTN_W = NA_HEADS * NA_HEAD_DIM
WIN_R = 8
WIN_C = 16
SSD_EXPAND = 2
D_INNER = SSD_EXPAND * D_MODEL
SSD_HEAD_DIM = 64
SSD_HEADS = D_INNER // SSD_HEAD_DIM
SSD_GROUPS = 8
HEADS_PER_GROUP = SSD_HEADS // SSD_GROUPS
D_STATE = 128
CONV_K = 5
CHUNK = 128
CONV_CH = D_INNER + 2 * SSD_GROUPS * D_STATE
NORM_GROUP = D_INNER // SSD_GROUPS
D_FF = 4 * D_MODEL
N_BRANCH = 2
N_IN = 3 * ATTN_W + D_INNER + CONV_CH + 2 * SSD_HEADS + N_BRANCH * D_MODEL
DN_ALPHA = (2 * DEPTH) ** 0.25
DN_BETA = (8 * DEPTH) ** -0.25
LN_EPS = 1e-5
RMS_EPS = 1e-5

kernel_name = "hybrid_natten_ssd_deepnorm_encoder"


def layer_norm(x, g, b):
    xf = x.astype(jnp.float32)
    mu = jnp.mean(xf, axis=-1, keepdims=True)
    var = jnp.mean(jnp.square(xf - mu), axis=-1, keepdims=True)
    return ((xf - mu) * lax.rsqrt(var + LN_EPS) * g + b).astype(x.dtype)


def neighbourhood_attention(q, k, v, rpb):
    bsz, s, h, dh = q.shape
    rows = s // GRID_W
    kr = min(WIN_R, rows)
    kc = min(WIN_C, GRID_W)
    cols = jnp.arange(GRID_W)
    col_start = jnp.clip(cols - kc // 2, 0, GRID_W - kc)
    col_idx = col_start[:, None] + jnp.arange(kc)[None, :]
    dc = col_idx - cols[:, None] + (WIN_C - 1)
    qg = jnp.moveaxis(q.reshape(bsz, rows, GRID_W, h, dh), 1, 0) * (dh ** -0.5)
    kg = k.reshape(bsz, rows, GRID_W, h, dh)
    vg = v.reshape(bsz, rows, GRID_W, h, dh)

    def row_block(args):
        r, q_row = args
        r0 = jnp.clip(r - kr // 2, 0, rows - kr)
        k_rows = lax.dynamic_slice_in_dim(kg, r0, kr, axis=1)
        v_rows = lax.dynamic_slice_in_dim(vg, r0, kr, axis=1)
        k_nb = k_rows[:, :, col_idx]
        v_nb = v_rows[:, :, col_idx]
        dr = r0 + jnp.arange(kr) - r + (WIN_R - 1)
        bias = rpb[:, dr[None, :, None], dc[:, None, :]].astype(jnp.float32)
        sc = jnp.einsum('bwhd,brwkhd->bhwrk', q_row, k_nb).astype(jnp.float32) + bias[None]
        p = jax.nn.softmax(sc.reshape(bsz, h, GRID_W, kr * kc), axis=-1)
        p = p.reshape(sc.shape).astype(v.dtype)
        return jnp.einsum('bhwrk,brwkhd->bwhd', p, v_nb)

    out = lax.map(row_block, (jnp.arange(rows), qg))
    return jnp.moveaxis(out, 0, 1).reshape(bsz, s, h * dh)


def centred_dwconv(u, w, b):
    pad = CONV_K // 2
    y = lax.conv_general_dilated(u, w[:, None, :], window_strides=(1,), padding=[(pad, pad)],
                                 dimension_numbers=('NWC', 'WIO', 'NWC'),
                                 feature_group_count=u.shape[-1])
    return y + b


def ssd_chunked(xh, dt, a, bmat, cmat):
    bsz, s, g, r, p = xh.shape
    n = bmat.shape[-1]
    nc = s // CHUNK
    xc = (xh * dt[..., None]).reshape(bsz, nc, CHUNK, g, r, p)
    adt = (dt * a).reshape(bsz, nc, CHUNK, g, r).transpose(0, 1, 3, 4, 2)
    a_cs = jnp.cumsum(adt, axis=-1)
    bc = bmat.reshape(bsz, nc, CHUNK, g, n)
    cc = cmat.reshape(bsz, nc, CHUNK, g, n)
    seg = a_cs[..., :, None] - a_cs[..., None, :]
    tril = jnp.tril(jnp.ones((CHUNK, CHUNK), dtype=bool))
    lmat = jnp.exp(jnp.where(tril, seg, -jnp.inf))
    cb = jnp.einsum('bclgn,bcsgn->bcgls', cc, bc)
    y_diag = jnp.einsum('bcgls,bcgrls,bcsgrp->bclgrp', cb, lmat, xc)
    decay_states = jnp.exp(a_cs[..., -1:] - a_cs)
    states = jnp.einsum('bclgn,bcgrl,bclgrp->bcgrpn', bc, decay_states, xc)
    chunk_decay = jnp.exp(a_cs[..., -1])

    def step(hs, inp):
        st, dec = inp
        return hs * dec[..., None, None] + st, hs

    h0 = jnp.zeros((bsz, g, r, p, n), dtype=states.dtype)
    _, prev = lax.scan(step, h0, (jnp.moveaxis(states, 1, 0), jnp.moveaxis(chunk_decay, 1, 0)))
    prev = jnp.moveaxis(prev, 0, 1)
    y_off = jnp.einsum('bclgn,bcgrpn,bcgrl->bclgrp', cc, prev, jnp.exp(a_cs))
    return (y_diag + y_off).reshape(bsz, s, g, r, p)


def hybrid_mixer(u, w_in, conv_w, conv_b, a_log, dt_bias, d_skip, ssd_norm_w, rpb,
                 w_attn_br, w_ssd_br, b_gate, w_o):
    bsz, s, _ = u.shape
    f32 = jnp.float32
    proj = u @ w_in
    offs = [ATTN_W, 2 * ATTN_W, 3 * ATTN_W, 3 * ATTN_W + D_INNER,
            3 * ATTN_W + D_INNER + CONV_CH, 3 * ATTN_W + D_INNER + CONV_CH + 2 * SSD_HEADS]
    q, k, v, z, xbc, dt_raw, gate = jnp.split(proj, offs, axis=-1)

    hs = (bsz, s, NA_HEADS, NA_HEAD_DIM)
    ya = neighbourhood_attention(q.reshape(hs), k.reshape(hs), v.reshape(hs), rpb) @ w_attn_br

    xbc = jax.nn.silu(centred_dwconv(xbc, conv_w, conv_b)).astype(f32)
    xs, bm, cm = jnp.split(xbc, [D_INNER, D_INNER + SSD_GROUPS * D_STATE], axis=-1)
    xs = xs.reshape(bsz, s, SSD_GROUPS, HEADS_PER_GROUP, SSD_HEAD_DIM)
    bm = bm.reshape(bsz, s, SSD_GROUPS, D_STATE)
    cm = cm.reshape(bsz, s, SSD_GROUPS, D_STATE)
    dt = jax.nn.softplus(dt_raw.astype(f32).reshape(bsz, s, 2, SSD_GROUPS, HEADS_PER_GROUP)
                         + dt_bias.astype(f32).reshape(2, SSD_GROUPS, HEADS_PER_GROUP))
    a = -jnp.exp(a_log.astype(f32)).reshape(2, SSD_GROUPS, HEADS_PER_GROUP)
    y_f = ssd_chunked(xs, dt[:, :, 0], a[0], bm, cm)
    flip = lambda t: jnp.flip(t, axis=1)
    y_b = flip(ssd_chunked(flip(xs), flip(dt[:, :, 1]), a[1], flip(bm), flip(cm)))
    y = y_f + y_b + xs * d_skip.astype(f32).reshape(SSD_GROUPS, HEADS_PER_GROUP)[..., None]
    y = y.reshape(bsz, s, SSD_GROUPS, NORM_GROUP) * jax.nn.silu(z.astype(f32)).reshape(bsz, s, SSD_GROUPS, NORM_GROUP)
    y = y * lax.rsqrt(jnp.mean(jnp.square(y), axis=-1, keepdims=True) + RMS_EPS)
    y = y * ssd_norm_w.astype(f32).reshape(SSD_GROUPS, NORM_GROUP)
    ys = y.reshape(bsz, s, D_INNER).astype(u.dtype) @ w_ssd_br

    gates = jax.nn.sigmoid((gate + b_gate).astype(f32)).reshape(bsz, s, N_BRANCH, D_MODEL).astype(u.dtype)
    merged = gates[:, :, 0] * ya + gates[:, :, 1] * ys
    return merged @ w_o


def sq_relu_mlp(x, w1, w2):
    return jnp.square(jax.nn.relu(x @ w1)) @ w2


def setup_inputs(seed: int = 0) -> dict:
    key = jax.random.key(seed)
    ks = jax.random.split(key, 24)
    L = DEPTH
    nrm = lambda k, shape, scale: jax.random.normal(k, shape, jnp.float32) * scale
    x = jax.random.normal(ks[0], (BATCH, SEQ, D_MODEL), jnp.float32)
    ln0_g = 1.0 + nrm(ks[1], (D_MODEL,), 0.02)
    ln0_b = nrm(ks[2], (D_MODEL,), 0.02)
    col_scale = jnp.concatenate([
        jnp.ones((2 * ATTN_W,), jnp.float32),
        jnp.full((ATTN_W,), DN_BETA, jnp.float32),
        jnp.ones((D_INNER,), jnp.float32),
        jnp.full((D_INNER,), DN_BETA, jnp.float32),
        jnp.ones((N_IN - 3 * ATTN_W - 2 * D_INNER,), jnp.float32)])
    w_in = nrm(ks[3], (L, D_MODEL, N_IN), D_MODEL ** -0.5) * col_scale
    conv_w = nrm(ks[4], (L, CONV_K, CONV_CH), CONV_K ** -0.5)
    conv_b = nrm(ks[5], (L, CONV_CH), 0.01)
    a_log = jnp.log(jax.random.uniform(ks[6], (L, 2, SSD_HEADS), jnp.float32, 1.0, 16.0))
    dt0 = jnp.exp(jax.random.uniform(ks[7], (L, 2, SSD_HEADS), jnp.float32, math.log(1e-3), math.log(0.1)))
    dt_bias = dt0 + jnp.log(-jnp.expm1(-dt0))
    d_skip = 1.0 + nrm(ks[8], (L, SSD_HEADS), 0.1)
    ssd_norm_w = 1.0 + nrm(ks[9], (L, D_INNER), 0.02)
    rpb = nrm(ks[10], (L, NA_HEADS, 2 * WIN_R - 1, 2 * WIN_C - 1), 0.02)
    w_attn_br = nrm(ks[11], (L, ATTN_W, D_MODEL), ATTN_W ** -0.5 * DN_BETA)
    w_ssd_br = nrm(ks[12], (L, D_INNER, D_MODEL), D_INNER ** -0.5 * DN_BETA)
    b_gate = nrm(ks[13], (L, N_BRANCH * D_MODEL), 0.1)
    w_o = nrm(ks[14], (L, D_MODEL, D_MODEL), D_MODEL ** -0.5 * DN_BETA)
    ln1_g = 1.0 + nrm(ks[15], (L, D_MODEL), 0.02)
    ln1_b = nrm(ks[16], (L, D_MODEL), 0.02)
    w_ff1 = nrm(ks[17], (L, D_MODEL, D_FF), D_MODEL ** -0.5 * DN_BETA)
    w_ff2 = nrm(ks[18], (L, D_FF, D_MODEL), D_FF ** -0.5 * DN_BETA)
    ln2_g = 1.0 + nrm(ks[19], (L, D_MODEL), 0.02)
    ln2_b = nrm(ks[20], (L, D_MODEL), 0.02)
    return {"x": x, "ln0_g": ln0_g, "ln0_b": ln0_b, "w_in": w_in, "conv_w": conv_w,
            "conv_b": conv_b, "a_log": a_log, "dt_bias": dt_bias, "d_skip": d_skip,
            "ssd_norm_w": ssd_norm_w, "rpb": rpb, "w_attn_br": w_attn_br, "w_ssd_br": w_ssd_br,
            "b_gate": b_gate, "w_o": w_o, "ln1_g": ln1_g, "ln1_b": ln1_b, "w_ff1": w_ff1,
            "w_ff2": w_ff2, "ln2_g": ln2_g, "ln2_b": ln2_b}


def reference(x, ln0_g, ln0_b, w_in, conv_w, conv_b, a_log, dt_bias, d_skip, ssd_norm_w, rpb,
              w_attn_br, w_ssd_br, b_gate, w_o, ln1_g, ln1_b, w_ff1, w_ff2, ln2_g, ln2_b):
    h = layer_norm(x, ln0_g, ln0_b)
    for l in range(DEPTH):
        mix = hybrid_mixer(h, w_in[l], conv_w[l], conv_b[l], a_log[l], dt_bias[l], d_skip[l],
                           ssd_norm_w[l], rpb[l], w_attn_br[l], w_ssd_br[l], b_gate[l], w_o[l])
        h = layer_norm(DN_ALPHA * h + mix, ln1_g[l], ln1_b[l])
        h = layer_norm(DN_ALPHA * h + sq_relu_mlp(h, w_ff1[l], w_ff2[l]), ln2_g[l], ln2_b[l])
    return h
```

```python
import functools
import math

import jax
import jax.numpy as jnp
from jax import lax
from jax.experimental import pallas as pl
from jax.experimental.pallas import tpu as pltpu

D_MODEL = 1024
DEPTH = 2
GRID_W = 64
NA_HEADS = 16
NA_HEAD_DIM = 64
ATTN_W = NA_HEADS * NA_HEAD_DIM
WIN_R = 8
WIN_C = 16
D_INNER = 2048
SSD_HEAD_DIM = 64
SSD_HEADS = D_INNER // SSD_HEAD_DIM
SSD_GROUPS = 8
HEADS_PER_GROUP = SSD_HEADS // SSD_GROUPS
D_STATE = 128
CONV_K = 5
CHUNK = 128
GROUP_W = D_INNER // SSD_GROUPS
D_FF = 4 * D_MODEL
DN_ALPHA = (2 * DEPTH) ** 0.25
LN_EPS = 1e-5
RMS_EPS = 1e-5

V7X_VMEM_BYTES = 64 * 1024 * 1024
VMEM_LIMIT = 56 * 1024 * 1024
LANES = 128
NEG_BIG = -1e30

F32 = jnp.float32
BF16 = jnp.bfloat16


def _cparams(*sem):
    return pltpu.CompilerParams(dimension_semantics=sem, vmem_limit_bytes=VMEM_LIMIT)


def _ln_rows(x, g, b):
    mu = jnp.mean(x, axis=-1, keepdims=True)
    xc = x - mu
    var = jnp.mean(xc * xc, axis=-1, keepdims=True)
    return xc * lax.rsqrt(var + LN_EPS) * g + b


def _ln_kernel(x_ref, g_ref, b_ref, of_ref, ob_ref):
    y = _ln_rows(x_ref[...], g_ref[...], b_ref[...])
    of_ref[...] = y
    ob_ref[...] = y.astype(BF16)


def layer_norm_call(x, g, b, tm=512):
    t, d = x.shape
    return pl.pallas_call(
        _ln_kernel,
        grid=(t // tm,),
        in_specs=[pl.BlockSpec((tm, d), lambda i: (i, 0)),
                  pl.BlockSpec((1, d), lambda i: (0, 0)),
                  pl.BlockSpec((1, d), lambda i: (0, 0))],
        out_specs=[pl.BlockSpec((tm, d), lambda i: (i, 0)),
                   pl.BlockSpec((tm, d), lambda i: (i, 0))],
        out_shape=[jax.ShapeDtypeStruct((t, d), F32), jax.ShapeDtypeStruct((t, d), BF16)],
        compiler_params=_cparams("parallel"),
        name="ln0",
    )(x, g.reshape(1, d), b.reshape(1, d))


def _mm_kernel(x_ref, w_ref, o_ref):
    o_ref[...] = jnp.dot(x_ref[...], w_ref[...], preferred_element_type=F32).astype(o_ref.dtype)


def matmul_call(x, w, out_dtype, tm, tn, name):
    t, k = x.shape
    n = w.shape[1]
    return pl.pallas_call(
        _mm_kernel,
        grid=(n // tn, t // tm),
        in_specs=[pl.BlockSpec((tm, k), lambda j, i: (i, 0)),
                  pl.BlockSpec((k, tn), lambda j, i: (0, j))],
        out_specs=pl.BlockSpec((tm, tn), lambda j, i: (i, j)),
        out_shape=jax.ShapeDtypeStruct((t, n), out_dtype),
        compiler_params=_cparams("parallel", "parallel"),
        name=name,
    )(x, w)


def _natten_kernel(q_ref, k_ref, v_ref, bias_ref, o_ref, *, rows):
    lane = lax.broadcasted_iota(jnp.int32, (GRID_W, LANES), 1)
    first = lane < NA_HEAD_DIM
    nkeys = WIN_R * GRID_W

    def row_body(r, carry):
        r0 = jnp.clip(r - WIN_R // 2, 0, rows - WIN_R)
        d0 = r0 - r + (WIN_R - 1)
        qs = pl.multiple_of(r * GRID_W, GRID_W)
        ks = pl.multiple_of(r0 * GRID_W, GRID_W)
        q = q_ref[pl.ds(qs, GRID_W), :] * jnp.asarray(NA_HEAD_DIM ** -0.5, BF16)
        kw = k_ref[pl.ds(ks, nkeys), :]
        vw = v_ref[pl.ds(ks, nkeys), :]
        outs = []
        for hh in range(2):
            keep = first if hh == 0 else jnp.logical_not(first)
            qh = jnp.where(keep, q, jnp.zeros_like(q))
            s = lax.dot_general(qh, kw, (((1,), (1,)), ((), ())), preferred_element_type=F32)
            bias = jnp.concatenate([bias_ref[hh, d0 + 2 * j] for j in range(WIN_R // 2)], axis=1)
            s = s + bias
            m = jnp.max(s, axis=-1, keepdims=True)
            p = jnp.exp(s - m)
            l = jnp.sum(p, axis=-1, keepdims=True)
            o = jnp.dot(p.astype(BF16), vw, preferred_element_type=F32)
            outs.append(o / l)
        o_ref[pl.ds(qs, GRID_W), :] = jnp.where(first, outs[0], outs[1]).astype(o_ref.dtype)
        return carry

    lax.fori_loop(0, rows, row_body, 0)


def natten_call(qkv, bias_tab):
    bsz, s, _ = qkv.shape
    rows = s // GRID_W
    npairs = NA_HEADS // 2
    blk = lambda off: pl.BlockSpec((None, s, LANES), lambda b, hp: (b, 0, off + hp))
    return pl.pallas_call(
        functools.partial(_natten_kernel, rows=rows),
        grid=(bsz, npairs),
        in_specs=[blk(0), blk(npairs), blk(2 * npairs),
                  pl.BlockSpec((2, 2 * WIN_R - 2, GRID_W, LANES), lambda b, hp: (hp, 0, 0, 0))],
        out_specs=pl.BlockSpec((None, s, LANES), lambda b, hp: (b, 0, hp)),
        out_shape=jax.ShapeDtypeStruct((bsz, s, ATTN_W), BF16),
        compiler_params=_cparams("parallel", "parallel"),
        name="natten",
    )(qkv, qkv, qkv, bias_tab)


def build_bias_table(rpb):
    w = jnp.arange(GRID_W)[:, None]
    kc = jnp.arange(GRID_W)[None, :]
    c0 = jnp.clip(w - WIN_C // 2, 0, GRID_W - WIN_C)
    valid = (kc >= c0) & (kc < c0 + WIN_C)
    dc = jnp.clip(kc - w + (WIN_C - 1), 0, 2 * WIN_C - 2)
    dense = jnp.where(valid[None, None], rpb[:, :, dc].astype(F32), NEG_BIG)
    return jnp.concatenate([dense[:, :-1], dense[:, 1:]], axis=-1)


def _silu(x):
    return x * jax.nn.sigmoid(x)


def _softplus(x):
    return jnp.maximum(x, 0.0) + jnp.log1p(jnp.exp(-jnp.abs(x)))


def _lane_scan(x, reverse):
    lane = lax.broadcasted_iota(jnp.int32, x.shape, 1)
    sh = 1
    while sh < CHUNK:
        if reverse:
            x = x + jnp.where(lane < CHUNK - sh, pltpu.roll(x, CHUNK - sh, 1), 0.0)
        else:
            x = x + jnp.where(lane >= sh, pltpu.roll(x, sh, 1), 0.0)
        sh *= 2
    return x


def _expand_heads(cols):
    lane = lax.broadcasted_iota(jnp.int32, (CHUNK, LANES), 1)
    first = lane < SSD_HEAD_DIM
    bc = [jnp.broadcast_to(c, (CHUNK, LANES)) for c in cols]
    return jnp.concatenate([jnp.where(first, bc[0], bc[1]), jnp.where(first, bc[2], bc[3])], axis=1)


def _expand_heads_row(rows):
    lane = lax.broadcasted_iota(jnp.int32, (1, LANES), 1)
    first = lane < SSD_HEAD_DIM
    return jnp.concatenate([jnp.where(first, rows[0], rows[1]), jnp.where(first, rows[2], rows[3])], axis=1)


def _ssd_kernel(xs_ref, bm_ref, cm_ref, z_ref, dtr_ref, cwx_ref, cwb_ref, cwc_ref, cbx_ref, cbb_ref, cbc_ref,
                pa_ref, pb_ref, dsk_ref, nw_ref, o_ref,
                pad_x, pad_b, pad_c, xc_s, bc_s, cc_s, bt_s, y_s, st_s, *, seq):
    nchunk = seq // CHUNK
    halo = 8

    def conv_into(src_ref, pad_ref, w_ref, b_ref, store):
        width = src_ref.shape[-1]
        pad_ref[pl.ds(0, halo), :] = jnp.zeros((halo, width), F32)
        pad_ref[pl.ds(halo + seq, halo), :] = jnp.zeros((halo, width), F32)
        pad_ref[pl.ds(halo, seq), :] = src_ref[...]
        span = CHUNK + 2 * halo

        def body(c, carry):
            t0 = pl.multiple_of(c * CHUNK, CHUNK)
            slab = pad_ref[pl.ds(t0, span), :]
            acc = jnp.broadcast_to(b_ref[...], (CHUNK, width))
            for k in range(CONV_K):
                d = k - CONV_K // 2
                shifted = slab if d == 0 else pltpu.roll(slab, (span - d) % span, 0)
                acc = acc + shifted[halo:halo + CHUNK, :] * w_ref[k:k + 1, :]
            store(t0, _silu(acc))
            return carry

        lax.fori_loop(0, nchunk, body, 0)

    def store_x(t0, v):
        xc_s[pl.ds(t0, CHUNK), :] = v

    def store_b(t0, v):
        bc_s[pl.ds(t0, CHUNK), :] = v.astype(BF16)
        bt_s[:, pl.ds(t0, CHUNK)] = v.T.astype(BF16)

    def store_c(t0, v):
        cc_s[pl.ds(t0, CHUNK), :] = v.astype(BF16)

    conv_into(xs_ref, pad_x, cwx_ref, cbx_ref, store_x)
    conv_into(bm_ref, pad_b, cwb_ref, cbb_ref, store_b)
    conv_into(cm_ref, pad_c, cwc_ref, cbc_ref, store_c)

    row_i = lax.broadcasted_iota(jnp.int32, (CHUNK, CHUNK), 0)
    col_i = lax.broadcasted_iota(jnp.int32, (CHUNK, CHUNK), 1)
    lane256 = lax.broadcasted_iota(jnp.int32, (CHUNK, GROUP_W), 1)
    head_masks = [(lane256 >= SSD_HEAD_DIM * r) & (lane256 < SSD_HEAD_DIM * (r + 1)) for r in range(HEADS_PER_GROUP)]
    a_rows = pa_ref[...]
    bias_rows = pb_ref[...]

    def direction_pass(direction):
        causal = (row_i >= col_i) if direction == 0 else (row_i <= col_i)
        st_s[...] = jnp.zeros((D_STATE, GROUP_W), F32)

        def body(i, carry):
            c = i if direction == 0 else nchunk - 1 - i
            t0 = pl.multiple_of(c * CHUNK, CHUNK)
            xk = xc_s[pl.ds(t0, CHUNK), :]
            bk = bc_s[pl.ds(t0, CHUNK), :]
            ck = cc_s[pl.ds(t0, CHUNK), :]
            btk = bt_s[:, pl.ds(t0, CHUNK)]
            dt = _softplus(dtr_ref[:, pl.ds(t0, CHUNK)] + bias_rows)
            cs = _lane_scan(dt * a_rows, reverse=(direction == 1))
            edge = CHUNK - 1 if direction == 0 else 0
            total = jnp.broadcast_to(cs[:, edge:edge + 1], cs.shape)
            w_state = dt * jnp.exp(total - cs)
            w_out = jnp.exp(cs)
            decay = jnp.exp(total)
            stack = jnp.concatenate([cs, w_state, w_out, jnp.zeros((CHUNK - 24, CHUNK), F32)], axis=0)
            cols = stack.T
            cb = lax.dot_general(ck, bk, (((1,), (1,)), ((), ())), preferred_element_type=F32)
            xb = xk.astype(BF16)
            ms, xparts = [], []
            for r in range(HEADS_PER_GROUP):
                j = direction * HEADS_PER_GROUP + r
                seg = cols[:, j:j + 1] - cs[j:j + 1, :]
                lm = jnp.exp(jnp.where(causal, seg, NEG_BIG))
                ms.append((cb * lm * dt[j:j + 1, :]).astype(BF16))
                xparts.append(jnp.where(head_masks[r], xb, jnp.zeros_like(xb)))
            y = jnp.dot(jnp.concatenate(ms, axis=1), jnp.concatenate(xparts, axis=0),
                        preferred_element_type=F32)
            j0 = direction * HEADS_PER_GROUP
            w_state_x = _expand_heads([cols[:, 8 + j0 + r:9 + j0 + r] for r in range(HEADS_PER_GROUP)])
            w_out_x = _expand_heads([cols[:, 16 + j0 + r:17 + j0 + r] for r in range(HEADS_PER_GROUP)])
            decay_x = _expand_heads_row([decay[j0 + r:j0 + r + 1, :] for r in range(HEADS_PER_GROUP)])
            prev = st_s[...]
            y = y + jnp.dot(ck, prev.astype(BF16), preferred_element_type=F32) * w_out_x
            new = jnp.dot(btk, (xk * w_state_x).astype(BF16), preferred_element_type=F32)
            st_s[...] = prev * decay_x + new
            if direction == 0:
                y_s[pl.ds(t0, CHUNK), :] = y
            else:
                y_s[pl.ds(t0, CHUNK), :] = y_s[pl.ds(t0, CHUNK), :] + y
            return carry

        lax.fori_loop(0, nchunk, body, 0)

    direction_pass(0)
    direction_pass(1)

    def fin(c, carry):
        t0 = pl.multiple_of(c * CHUNK, CHUNK)
        y = y_s[pl.ds(t0, CHUNK), :] + xc_s[pl.ds(t0, CHUNK), :] * dsk_ref[...]
        y = y * _silu(z_ref[pl.ds(t0, CHUNK), :])
        y = y * lax.rsqrt(jnp.mean(y * y, axis=-1, keepdims=True) + RMS_EPS)
        o_ref[pl.ds(t0, CHUNK), :] = (y * nw_ref[...]).astype(o_ref.dtype)
        return carry

    lax.fori_loop(0, nchunk, fin, 0)


def ssd_call(pf, dtr, cw, cb, pa, pb, dsk, nw):
    bsz, s, _ = pf.shape
    g_ = SSD_GROUPS
    xo, bo, co, zo = 0, D_INNER // D_STATE, (D_INNER + g_ * D_STATE) // D_STATE, (D_INNER + 2 * g_ * D_STATE) // GROUP_W
    cwx, cwb, cwc = cw[:, :D_INNER], cw[:, D_INNER:D_INNER + g_ * D_STATE], cw[:, D_INNER + g_ * D_STATE:]
    cbx, cbb, cbc = cb[:, :D_INNER], cb[:, D_INNER:D_INNER + g_ * D_STATE], cb[:, D_INNER + g_ * D_STATE:]
    wide = lambda off: pl.BlockSpec((None, s, GROUP_W), lambda b, g: (b, 0, off + g))
    narrow = lambda off: pl.BlockSpec((None, s, D_STATE), lambda b, g: (b, 0, off + g))
    pw = lambda rows, width: pl.BlockSpec((rows, width), lambda b, g: (0, g))
    return pl.pallas_call(
        functools.partial(_ssd_kernel, seq=s),
        grid=(bsz, g_),
        in_specs=[wide(xo), narrow(bo), narrow(co), wide(zo),
                  pl.BlockSpec((None, None, 2 * HEADS_PER_GROUP, s), lambda b, g: (b, g, 0, 0)),
                  pw(CONV_K, GROUP_W), pw(CONV_K, D_STATE), pw(CONV_K, D_STATE),
                  pw(1, GROUP_W), pw(1, D_STATE), pw(1, D_STATE),
                  pl.BlockSpec((None, 2 * HEADS_PER_GROUP, LANES), lambda b, g: (g, 0, 0)),
                  pl.BlockSpec((None, 2 * HEADS_PER_GROUP, LANES), lambda b, g: (g, 0, 0)),
                  pw(1, GROUP_W), pw(1, GROUP_W)],
        out_specs=pl.BlockSpec((None, s, GROUP_W), lambda b, g: (b, 0, g)),
        out_shape=jax.ShapeDtypeStruct((bsz, s, D_INNER), BF16),
        scratch_shapes=[pltpu.VMEM((s + 16, GROUP_W), F32), pltpu.VMEM((s + 16, D_STATE), F32),
                        pltpu.VMEM((s + 16, D_STATE), F32),
                        pltpu.VMEM((s, GROUP_W), F32), pltpu.VMEM((s, D_STATE), BF16),
                        pltpu.VMEM((s, D_STATE), BF16), pltpu.VMEM((D_STATE, s), BF16),
                        pltpu.VMEM((s, GROUP_W), F32), pltpu.VMEM((D_STATE, GROUP_W), F32)],
        compiler_params=_cparams("parallel", "parallel"),
        name="ssd",
    )(pf, pf, pf, pf, dtr, cwx, cwb, cwc, cbx, cbb, cbc, pa, pb, dsk, nw)


def _merge_kernel(att_ref, yn_ref, g0_ref, g1_ref, bg0_ref, bg1_ref, h_ref, wa_ref, ws_ref, wo_ref,
                  lg_ref, lb_ref, of_ref, ob_ref):
    ya = jnp.dot(att_ref[...], wa_ref[...], preferred_element_type=F32)
    ys = jnp.dot(yn_ref[...], ws_ref[...], preferred_element_type=F32)
    merged = jax.nn.sigmoid(g0_ref[...] + bg0_ref[...]) * ya + jax.nn.sigmoid(g1_ref[...] + bg1_ref[...]) * ys
    mix = jnp.dot(merged.astype(BF16), wo_ref[...], preferred_element_type=F32)
    y = _ln_rows(DN_ALPHA * h_ref[...] + mix, lg_ref[...], lb_ref[...])
    of_ref[...] = y
    ob_ref[...] = y.astype(BF16)


def merge_call(att, yn, pf, b_gate, h, wa, ws, wo, lg, lb, tm=256):
    t, d = h.shape
    gate_off = (pf.shape[1] - 2 * d) // d
    row = lambda width: pl.BlockSpec((tm, width), lambda i: (i, 0))
    const = lambda a: pl.BlockSpec(a.shape, lambda i: (0,) * a.ndim)
    bg = b_gate.reshape(1, 2 * d)
    return pl.pallas_call(
        _merge_kernel,
        grid=(t // tm,),
        in_specs=[row(d), row(yn.shape[1]),
                  pl.BlockSpec((tm, d), lambda i: (i, gate_off)),
                  pl.BlockSpec((tm, d), lambda i: (i, gate_off + 1)),
                  pl.BlockSpec((1, d), lambda i: (0, 0)), pl.BlockSpec((1, d), lambda i: (0, 1)),
                  row(d), const(wa), const(ws), const(wo),
                  pl.BlockSpec((1, d), lambda i: (0, 0)), pl.BlockSpec((1, d), lambda i: (0, 0))],
        out_specs=[row(d), row(d)],
        out_shape=[jax.ShapeDtypeStruct((t, d), F32), jax.ShapeDtypeStruct((t, d), BF16)],
        compiler_params=_cparams("parallel"),
        name="merge",
    )(att, yn, pf, pf, bg, bg, h, wa, ws, wo, lg.reshape(1, d), lb.reshape(1, d))


def _ffn_kernel(hb_ref, h_ref, w1_ref, w2_ref, lg_ref, lb_ref, of_ref, ob_ref, *, nsplit):
    hb = hb_ref[...]
    d_ff = w1_ref.shape[1]
    step = d_ff // nsplit
    acc = DN_ALPHA * h_ref[...]
    for j in range(nsplit):
        a = jnp.dot(hb, w1_ref[:, j * step:(j + 1) * step], preferred_element_type=F32)
        a = jnp.square(jnp.maximum(a, 0.0)).astype(BF16)
        acc = acc + jnp.dot(a, w2_ref[j * step:(j + 1) * step, :], preferred_element_type=F32)
    y = _ln_rows(acc, lg_ref[...], lb_ref[...])
    of_ref[...] = y
    ob_ref[...] = y.astype(BF16)


def ffn_call(hb, h, w1, w2, lg, lb, tm=256, nsplit=4):
    t, d = h.shape
    row = lambda: pl.BlockSpec((tm, d), lambda i: (i, 0))
    const = lambda a: pl.BlockSpec(a.shape, lambda i: (0,) * a.ndim)
    vec = pl.BlockSpec((1, d), lambda i: (0, 0))
    return pl.pallas_call(
        functools.partial(_ffn_kernel, nsplit=nsplit),
        grid=(t // tm,),
        in_specs=[row(), row(), const(w1), const(w2), vec, vec],
        out_specs=[row(), row()],
        out_shape=[jax.ShapeDtypeStruct((t, d), F32), jax.ShapeDtypeStruct((t, d), BF16)],
        compiler_params=_cparams("parallel"),
        name="ffn",
    )(hb, h, w1, w2, lg.reshape(1, d), lb.reshape(1, d))


def _split_w_in(w):
    o = 3 * ATTN_W
    conv_ch = D_INNER + 2 * SSD_GROUPS * D_STATE
    w_qkv = w[:, :o]
    w_z = w[:, o:o + D_INNER]
    w_xbc = w[:, o + D_INNER:o + D_INNER + conv_ch]
    w_dt = w[:, o + D_INNER + conv_ch:o + D_INNER + conv_ch + 2 * SSD_HEADS]
    w_gate = w[:, o + D_INNER + conv_ch + 2 * SSD_HEADS:]
    w_dt = w_dt.reshape(-1, 2, SSD_GROUPS, HEADS_PER_GROUP).transpose(0, 2, 1, 3).reshape(-1, 2 * SSD_HEADS)
    w_dt = jnp.pad(w_dt, ((0, 0), (0, LANES - 2 * SSD_HEADS)))
    return (w_qkv.astype(BF16), jnp.concatenate([w_xbc, w_z, w_gate], axis=1).astype(BF16), w_dt.astype(BF16))


def _group_rows(p):
    p = p.astype(F32).reshape(2, SSD_GROUPS, HEADS_PER_GROUP).transpose(1, 0, 2).reshape(SSD_GROUPS, 2 * HEADS_PER_GROUP)
    return jnp.broadcast_to(p[:, :, None], (SSD_GROUPS, 2 * HEADS_PER_GROUP, LANES))


def _mixer_layer(h, hb, bsz, s, w_in, conv_w, conv_b, a_log, dt_bias, d_skip, ssd_norm_w, rpb,
                 w_attn_br, w_ssd_br, b_gate, w_o, ln_g, ln_b):
    t = bsz * s
    w_qkv, w_f, w_dt = _split_w_in(w_in)
    qkv = matmul_call(hb, w_qkv, BF16, 1024, 1024, "proj_qkv")
    pf = matmul_call(hb, w_f, F32, 1024, 1024, "proj_f32")
    dt_raw = matmul_call(hb, w_dt, F32, 1024, LANES, "proj_dt")

    att = natten_call(qkv.reshape(bsz, s, 3 * ATTN_W), build_bias_table(rpb))

    dtr = dt_raw[:, :2 * SSD_HEADS].reshape(bsz, s, SSD_GROUPS, 2 * HEADS_PER_GROUP).transpose(0, 2, 3, 1)
    pa = _group_rows(-jnp.exp(a_log.astype(F32)))
    pb = _group_rows(dt_bias)
    dsk = jnp.repeat(d_skip.astype(F32), SSD_HEAD_DIM).reshape(1, D_INNER)
    yn = ssd_call(pf.reshape(bsz, s, -1), dtr, conv_w.astype(F32), conv_b.astype(F32).reshape(1, -1), pa, pb, dsk,
                  ssd_norm_w.astype(F32).reshape(1, D_INNER))

    return merge_call(att.reshape(t, ATTN_W), yn.reshape(t, D_INNER), pf, b_gate.astype(F32), h,
                      w_attn_br.astype(BF16), w_ssd_br.astype(BF16), w_o.astype(BF16), ln_g, ln_b)


def kernel(x, ln0_g, ln0_b, w_in, conv_w, conv_b, a_log, dt_bias, d_skip, ssd_norm_w, rpb, w_attn_br, w_ssd_br,
           b_gate, w_o, ln1_g, ln1_b, w_ff1, w_ff2, ln2_g, ln2_b):
    bsz, s, d = x.shape
    h, hb = layer_norm_call(x.reshape(bsz * s, d), ln0_g, ln0_b)
    for l in range(w_in.shape[0]):
        h, hb = _mixer_layer(h, hb, bsz, s, w_in[l], conv_w[l], conv_b[l], a_log[l], dt_bias[l], d_skip[l],
                             ssd_norm_w[l], rpb[l], w_attn_br[l], w_ssd_br[l], b_gate[l], w_o[l], ln1_g[l], ln1_b[l])
        h, hb = ffn_call(hb, h, w_ff1[l].astype(BF16), w_ff2[l].astype(BF16), ln2_g[l], ln2_b[l])
    return h.reshape(bsz, s, d)
```

```python
import functools
import math

import jax
import jax.numpy as jnp
import numpy as np
from jax import lax
from jax.experimental import pallas as pl
from jax.experimental.pallas import tpu as pltpu

D_MODEL = 1024
DEPTH = 2
GRID_W = 64
NA_HEADS = 16
NA_HEAD_DIM = 64
ATTN_W = NA_HEADS * NA_HEAD_DIM
WIN_R = 8
WIN_C = 16
D_INNER = 2048
SSD_HEAD_DIM = 64
SSD_HEADS = D_INNER // SSD_HEAD_DIM
SSD_GROUPS = 8
HEADS_PER_GROUP = SSD_HEADS // SSD_GROUPS
D_STATE = 128
CONV_K = 5
CHUNK = 128
GROUP_W = D_INNER // SSD_GROUPS
D_FF = 4 * D_MODEL
DN_ALPHA = (2 * DEPTH) ** 0.25
LN_EPS = 1e-5
RMS_EPS = 1e-5

V7X_VMEM_BYTES = 64 * 1024 * 1024
VMEM_LIMIT = 56 * 1024 * 1024
LANES = 128
NEG_BIG = -1e30

F32 = jnp.float32
BF16 = jnp.bfloat16


def _cparams(*sem):
    return pltpu.CompilerParams(dimension_semantics=sem, vmem_limit_bytes=VMEM_LIMIT)


def _ln_rows(x, g, b):
    mu = jnp.mean(x, axis=-1, keepdims=True)
    xc = x - mu
    var = jnp.mean(xc * xc, axis=-1, keepdims=True)
    return xc * lax.rsqrt(var + LN_EPS) * g + b


def _ln_kernel(x_ref, g_ref, b_ref, of_ref, ob_ref):
    y = _ln_rows(x_ref[...], g_ref[...], b_ref[...])
    of_ref[...] = y
    ob_ref[...] = y.astype(BF16)


def layer_norm_call(x, g, b, tm=512):
    t, d = x.shape
    return pl.pallas_call(
        _ln_kernel,
        grid=(t // tm,),
        in_specs=[pl.BlockSpec((tm, d), lambda i: (i, 0)),
                  pl.BlockSpec((1, d), lambda i: (0, 0)),
                  pl.BlockSpec((1, d), lambda i: (0, 0))],
        out_specs=[pl.BlockSpec((tm, d), lambda i: (i, 0)),
                   pl.BlockSpec((tm, d), lambda i: (i, 0))],
        out_shape=[jax.ShapeDtypeStruct((t, d), F32), jax.ShapeDtypeStruct((t, d), BF16)],
        compiler_params=_cparams("parallel"),
        name="ln0",
    )(x, g.reshape(1, d), b.reshape(1, d))


def _mm_kernel(x_ref, w_ref, o_ref):
    o_ref[...] = jnp.dot(x_ref[...], w_ref[...], preferred_element_type=F32).astype(o_ref.dtype)


def matmul_call(x, w, out_dtype, tm, tn, name):
    t, k = x.shape
    n = w.shape[1]
    return pl.pallas_call(
        _mm_kernel,
        grid=(n // tn, t // tm),
        in_specs=[pl.BlockSpec((tm, k), lambda j, i: (i, 0)),
                  pl.BlockSpec((k, tn), lambda j, i: (0, j))],
        out_specs=pl.BlockSpec((tm, tn), lambda j, i: (i, j)),
        out_shape=jax.ShapeDtypeStruct((t, n), out_dtype),
        compiler_params=_cparams("parallel", "parallel"),
        name=name,
    )(x, w)


Q_GROUP = 4
K_SLAB = Q_GROUP + WIN_R


def _slab_start(j, rows):
    return jnp.clip(Q_GROUP * j - WIN_R // 2, 0, rows - K_SLAB)


def _natten_kernel(q_ref, k_ref, v_ref, bias_ref, o_ref, *, rows):
    ngroups = rows // Q_GROUP
    nq = Q_GROUP * GRID_W
    nk = K_SLAB * GRID_W
    lane = lax.broadcasted_iota(jnp.int32, (nq, LANES), 1)
    first = lane < NA_HEAD_DIM

    def group_body(j, carry):
        variant = (j > 0).astype(jnp.int32) + (j == ngroups - 1).astype(jnp.int32)
        qs = pl.multiple_of(j * nq, nq)
        ks = pl.multiple_of(_slab_start(j, rows) * GRID_W, GRID_W)
        q = q_ref[pl.ds(qs, nq), :] * jnp.asarray(NA_HEAD_DIM ** -0.5, BF16)
        kw = k_ref[pl.ds(ks, nk), :]
        vw = v_ref[pl.ds(ks, nk), :]
        outs = []
        for hh in range(2):
            keep = first if hh == 0 else jnp.logical_not(first)
            qh = jnp.where(keep, q, jnp.zeros_like(q))
            s = lax.dot_general(qh, kw, (((1,), (1,)), ((), ())), preferred_element_type=F32)
            s = s + bias_ref[variant, hh]
            m = jnp.max(s, axis=-1, keepdims=True)
            p = jnp.exp(s - m)
            l = jnp.sum(p, axis=-1, keepdims=True)
            o = jnp.dot(p.astype(BF16), vw, preferred_element_type=F32)
            outs.append(o / l)
        o_ref[pl.ds(qs, nq), :] = jnp.where(first, outs[0], outs[1]).astype(o_ref.dtype)
        return carry

    lax.fori_loop(0, ngroups, group_body, 0, unroll=2)


def natten_call(qkv, bias_tab):
    bsz, s, _ = qkv.shape
    rows = s // GRID_W
    assert rows % Q_GROUP == 0 and rows >= K_SLAB + Q_GROUP
    npairs = NA_HEADS // 2
    blk = lambda off: pl.BlockSpec((None, s, LANES), lambda hp, b: (b, 0, off + hp))
    return pl.pallas_call(
        functools.partial(_natten_kernel, rows=rows),
        grid=(npairs, bsz),
        in_specs=[blk(0), blk(npairs), blk(2 * npairs),
                  pl.BlockSpec((3, 2) + bias_tab.shape[2:], lambda hp, b: (0, hp, 0, 0))],
        out_specs=pl.BlockSpec((None, s, LANES), lambda hp, b: (b, 0, hp)),
        out_shape=jax.ShapeDtypeStruct((bsz, s, ATTN_W), BF16),
        compiler_params=_cparams("parallel", "parallel"),
        name="natten",
    )(qkv, qkv, qkv, bias_tab)


def build_bias_table(rpb, rows):
    ngroups = rows // Q_GROUP
    pats = []
    for j in range(ngroups):
        ws = min(max(Q_GROUP * j - WIN_R // 2, 0), rows - K_SLAB)
        onehot = np.zeros((Q_GROUP, K_SLAB, 2 * WIN_R - 1), np.float32)
        for a in range(Q_GROUP):
            r = Q_GROUP * j + a
            r0 = min(max(r - WIN_R // 2, 0), rows - WIN_R)
            for i in range(K_SLAB):
                if r0 <= ws + i < r0 + WIN_R:
                    onehot[a, i, ws + i - r + WIN_R - 1] = 1.0
        pats.append(onehot)
    assert all(np.array_equal(pats[1], p) for p in pats[1:-1])
    row_sel = np.stack([pats[0], pats[1], pats[-1]])
    w = np.arange(GRID_W)[:, None]
    kc = np.arange(GRID_W)[None, :]
    c0 = np.clip(w - WIN_C // 2, 0, GRID_W - WIN_C)
    col_ok = (kc >= c0) & (kc < c0 + WIN_C)
    col_sel = np.zeros((GRID_W, GRID_W, 2 * WIN_C - 1), np.float32)
    col_sel[w, kc, np.clip(kc - w + WIN_C - 1, 0, 2 * WIN_C - 2)] = col_ok
    raw = jnp.einsum("vair,hrd,wkd->vhawik", row_sel, rpb.astype(F32), col_sel, precision=lax.Precision.HIGHEST)
    ok = (row_sel.sum(-1) > 0)[:, None, :, None, :, None] & col_ok[None, None, None, :, None, :]
    tab = jnp.where(ok, raw, NEG_BIG)
    return tab.reshape(3, rpb.shape[0], Q_GROUP * GRID_W, K_SLAB * GRID_W)


def _silu(x):
    return x * jax.nn.sigmoid(x)


def _softplus(x):
    return jnp.maximum(x, 0.0) + jnp.log1p(jnp.exp(-jnp.abs(x)))


def _chunk_scan(x, reverse):
    width = x.shape[1]
    lane = lax.broadcasted_iota(jnp.int32, x.shape, 1) & (CHUNK - 1)
    sh = 1
    while sh < CHUNK:
        if reverse:
            x = x + jnp.where(lane < CHUNK - sh, pltpu.roll(x, width - sh, 1), 0.0)
        else:
            x = x + jnp.where(lane >= sh, pltpu.roll(x, sh, 1), 0.0)
        sh *= 2
    return x


def _expand_heads_row(rows):
    lane = lax.broadcasted_iota(jnp.int32, (1, LANES), 1)
    first = lane < SSD_HEAD_DIM
    return jnp.concatenate([jnp.where(first, rows[0], rows[1]), jnp.where(first, rows[2], rows[3])], axis=1)


def _ssd_kernel(xs_ref, bm_ref, cm_ref, z_ref, dtr_ref, cwx_ref, cwb_ref, cwc_ref, cbx_ref, cbb_ref, cbc_ref,
                pa_ref, pb_ref, dsk_ref, nw_ref, o_ref,
                pad_x, pad_b, pad_c, xbd_s, bc_s, cc_s, bt_s, cb_s, rows_s, cols_s, y_s, st_s,
                *, seq):
    nchunk = seq // CHUNK
    halo = 8
    nh = HEADS_PER_GROUP
    lane256 = lax.broadcasted_iota(jnp.int32, (CHUNK, GROUP_W), 1)
    head_masks = [(lane256 >= SSD_HEAD_DIM * r) & (lane256 < SSD_HEAD_DIM * (r + 1)) for r in range(nh)]

    def conv_into(src_ref, pad_ref, w_ref, b_ref, store):
        width = src_ref.shape[-1]
        pad_ref[pl.ds(0, halo), :] = jnp.zeros((halo, width), F32)
        pad_ref[pl.ds(halo + seq, halo), :] = jnp.zeros((halo, width), F32)
        pad_ref[pl.ds(halo, seq), :] = src_ref[...]
        span = CHUNK + 2 * halo

        def body(c, carry):
            t0 = pl.multiple_of(c * CHUNK, CHUNK)
            slab = pad_ref[pl.ds(t0, span), :]
            acc = jnp.broadcast_to(b_ref[...], (CHUNK, width))
            for k in range(CONV_K):
                d = k - CONV_K // 2
                shifted = slab if d == 0 else pltpu.roll(slab, (span - d) % span, 0)
                acc = acc + shifted[halo:halo + CHUNK, :] * w_ref[k:k + 1, :]
            store(c, t0, _silu(acc))
            return carry

        lax.fori_loop(0, nchunk, body, 0)

    def store_x(c, t0, v):
        y_s[pl.ds(t0, CHUNK), :] = v * dsk_ref[...]
        vb = v.astype(BF16)
        base = pl.multiple_of(c * (nh * CHUNK), nh * CHUNK)
        for r in range(nh):
            xbd_s[pl.ds(base + r * CHUNK, CHUNK), :] = jnp.where(head_masks[r], vb, jnp.zeros_like(vb))

    def store_b(c, t0, v):
        bc_s[pl.ds(t0, CHUNK), :] = v.astype(BF16)
        bt_s[:, pl.ds(t0, CHUNK)] = v.T.astype(BF16)

    def store_c(c, t0, v):
        cc_s[pl.ds(t0, CHUNK), :] = v.astype(BF16)

    conv_into(bm_ref, pad_b, cwb_ref, cbb_ref, store_b)
    conv_into(cm_ref, pad_c, cwc_ref, cbc_ref, store_c)
    conv_into(xs_ref, pad_x, cwx_ref, cbx_ref, store_x)

    a_rows = jnp.concatenate([pa_ref[...]] * nchunk, axis=1)
    bias_rows = jnp.concatenate([pb_ref[...]] * nchunk, axis=1)
    is_bwd = lax.broadcasted_iota(jnp.int32, (2 * nh, seq), 0) >= nh
    dt = _softplus(dtr_ref[...] + bias_rows)
    adt = dt * a_rows
    pre = _chunk_scan(adt, reverse=False)
    suf = _chunk_scan(adt, reverse=True)
    cs = jnp.where(is_bwd, suf, pre)
    total = pre + suf - adt
    rows_s[0] = cs - jnp.log(dt)
    rows_s[1] = jnp.exp(total)
    rows_s[2] = dt * jnp.exp(total - cs)
    rows_s[3] = cs

    def prep(c, carry):
        t0 = pl.multiple_of(c * CHUNK, CHUNK)
        stack = jnp.concatenate([rows_s[3, :, pl.ds(t0, CHUNK)], jnp.zeros((CHUNK - 2 * nh, CHUNK), F32)], axis=0)
        cols_s[pl.ds(t0, CHUNK), :] = stack.T
        cb_s[pl.ds(t0, CHUNK), :] = lax.dot_general(cc_s[pl.ds(t0, CHUNK), :], bc_s[pl.ds(t0, CHUNK), :],
                                                    (((1,), (1,)), ((), ())), preferred_element_type=F32)
        return carry

    lax.fori_loop(0, nchunk, prep, 0, unroll=4)

    row_i = lax.broadcasted_iota(jnp.int32, (CHUNK, CHUNK), 0)
    col_i = lax.broadcasted_iota(jnp.int32, (CHUNK, CHUNK), 1)
    st_s[...] = jnp.zeros(st_s.shape, F32)

    def scan_step(direction, c):
        causal = (row_i >= col_i) if direction == 0 else (row_i <= col_i)
        t0 = pl.multiple_of(c * CHUNK, CHUNK)
        ck = cc_s[pl.ds(t0, CHUNK), :]
        cb = cb_s[pl.ds(t0, CHUNK), :]
        cols = cols_s[pl.ds(t0, CHUNK), :]
        bt = bt_s[:, pl.ds(t0, CHUNK)]
        src = rows_s[0, :, pl.ds(t0, CHUNK)]
        decay = rows_s[1, :, pl.ds(t0, CHUNK)]
        w_state = rows_s[2, :, pl.ds(t0, CHUNK)]
        j0 = direction * nh
        ms, bts, w_out = [], [], []
        for r in range(nh):
            cs_col = jnp.broadcast_to(cols[:, j0 + r:j0 + r + 1], (CHUNK, CHUNK))
            seg = cs_col - src[j0 + r:j0 + r + 1, :]
            ms.append((cb * jnp.exp(jnp.where(causal, seg, NEG_BIG))).astype(BF16))
            bts.append((bt * w_state[j0 + r:j0 + r + 1, :]).astype(BF16))
            w_out.append(jnp.exp(cs_col))
        lhs = jnp.concatenate([jnp.concatenate(ms, axis=1), jnp.concatenate(bts, axis=1)], axis=0)
        xbd = xbd_s[pl.ds(pl.multiple_of(c * (nh * CHUNK), nh * CHUNK), nh * CHUNK), :]
        both = jnp.dot(lhs, xbd, preferred_element_type=F32)
        first = col_i < SSD_HEAD_DIM
        w_out_x = jnp.concatenate([jnp.where(first, w_out[0], w_out[1]), jnp.where(first, w_out[2], w_out[3])], axis=1)
        prev = st_s[direction]
        y = both[:CHUNK] + jnp.dot(ck, prev.astype(BF16), preferred_element_type=F32) * w_out_x
        decay_x = _expand_heads_row([decay[j0 + r:j0 + r + 1, :] for r in range(nh)])
        st_s[direction] = prev * decay_x + both[CHUNK:]
        y_s[pl.ds(t0, CHUNK), :] = y_s[pl.ds(t0, CHUNK), :] + y

    def scan_body(i, carry):
        scan_step(0, i)
        scan_step(1, nchunk - 1 - i)
        return carry

    lax.fori_loop(0, nchunk, scan_body, 0, unroll=2)

    def fin(c, carry):
        t0 = pl.multiple_of(c * CHUNK, CHUNK)
        y = y_s[pl.ds(t0, CHUNK), :] * _silu(z_ref[pl.ds(t0, CHUNK), :])
        y = y * lax.rsqrt(jnp.mean(y * y, axis=-1, keepdims=True) + RMS_EPS)
        o_ref[pl.ds(t0, CHUNK), :] = (y * nw_ref[...]).astype(o_ref.dtype)
        return carry

    lax.fori_loop(0, nchunk, fin, 0, unroll=2)


def ssd_call(pf, dtr, cw, cb, pa, pb, dsk, nw):
    bsz, s, _ = pf.shape
    g_ = SSD_GROUPS
    nh = HEADS_PER_GROUP
    xo, bo, co, zo = 0, D_INNER // D_STATE, (D_INNER + g_ * D_STATE) // D_STATE, (D_INNER + 2 * g_ * D_STATE) // GROUP_W
    cwx, cwb, cwc = cw[:, :D_INNER], cw[:, D_INNER:D_INNER + g_ * D_STATE], cw[:, D_INNER + g_ * D_STATE:]
    cbx, cbb, cbc = cb[:, :D_INNER], cb[:, D_INNER:D_INNER + g_ * D_STATE], cb[:, D_INNER + g_ * D_STATE:]
    wide = lambda off: pl.BlockSpec((None, s, GROUP_W), lambda b, g: (b, 0, off + g))
    narrow = lambda off: pl.BlockSpec((None, s, D_STATE), lambda b, g: (b, 0, off + g))
    pw = lambda rows, width: pl.BlockSpec((rows, width), lambda b, g: (0, g))
    return pl.pallas_call(
        functools.partial(_ssd_kernel, seq=s),
        grid=(bsz, g_),
        in_specs=[wide(xo), narrow(bo), narrow(co), wide(zo),
                  pl.BlockSpec((None, None, 2 * nh, s), lambda b, g: (b, g, 0, 0)),
                  pw(CONV_K, GROUP_W), pw(CONV_K, D_STATE), pw(CONV_K, D_STATE),
                  pw(1, GROUP_W), pw(1, D_STATE), pw(1, D_STATE),
                  pl.BlockSpec((None, 2 * nh, LANES), lambda b, g: (g, 0, 0)),
                  pl.BlockSpec((None, 2 * nh, LANES), lambda b, g: (g, 0, 0)),
                  pw(1, GROUP_W), pw(1, GROUP_W)],
        out_specs=pl.BlockSpec((None, s, GROUP_W), lambda b, g: (b, 0, g)),
        out_shape=jax.ShapeDtypeStruct((bsz, s, D_INNER), BF16),
        scratch_shapes=[pltpu.VMEM((s + 16, GROUP_W), F32),
                        pltpu.VMEM((s + 16, D_STATE), F32),
                        pltpu.VMEM((s + 16, D_STATE), F32),
                        pltpu.VMEM((nh * s, GROUP_W), BF16),
                        pltpu.VMEM((s, D_STATE), BF16),
                        pltpu.VMEM((s, D_STATE), BF16),
                        pltpu.VMEM((D_STATE, s), BF16),
                        pltpu.VMEM((s, CHUNK), F32),
                        pltpu.VMEM((4, 2 * nh, s), F32),
                        pltpu.VMEM((s, LANES), F32),
                        pltpu.VMEM((s, GROUP_W), F32),
                        pltpu.VMEM((2, D_STATE, GROUP_W), F32)],
        compiler_params=_cparams("parallel", "parallel"),
        name="ssd",
    )(pf, pf, pf, pf, dtr, cwx, cwb, cwc, cbx, cbb, cbc, pa, pb, dsk, nw)


def _merge_kernel(att_ref, yn_ref, g0_ref, g1_ref, bg0_ref, bg1_ref, h_ref, wa_ref, ws_ref, wo_ref,
                  lg_ref, lb_ref, of_ref, ob_ref):
    ya = jnp.dot(att_ref[...], wa_ref[...], preferred_element_type=F32)
    ys = jnp.dot(yn_ref[...], ws_ref[...], preferred_element_type=F32)
    merged = jax.nn.sigmoid(g0_ref[...] + bg0_ref[...]) * ya + jax.nn.sigmoid(g1_ref[...] + bg1_ref[...]) * ys
    mix = jnp.dot(merged.astype(BF16), wo_ref[...], preferred_element_type=F32)
    y = _ln_rows(DN_ALPHA * h_ref[...] + mix, lg_ref[...], lb_ref[...])
    of_ref[...] = y
    ob_ref[...] = y.astype(BF16)


def merge_call(att, yn, pf, b_gate, h, wa, ws, wo, lg, lb, tm=256):
    t, d = h.shape
    gate_off = (pf.shape[1] - 2 * d) // d
    row = lambda width: pl.BlockSpec((tm, width), lambda i: (i, 0))
    const = lambda a: pl.BlockSpec(a.shape, lambda i: (0,) * a.ndim)
    bg = b_gate.reshape(1, 2 * d)
    return pl.pallas_call(
        _merge_kernel,
        grid=(t // tm,),
        in_specs=[row(d), row(yn.shape[1]),
                  pl.BlockSpec((tm, d), lambda i: (i, gate_off)),
                  pl.BlockSpec((tm, d), lambda i: (i, gate_off + 1)),
                  pl.BlockSpec((1, d), lambda i: (0, 0)), pl.BlockSpec((1, d), lambda i: (0, 1)),
                  row(d), const(wa), const(ws), const(wo),
                  pl.BlockSpec((1, d), lambda i: (0, 0)), pl.BlockSpec((1, d), lambda i: (0, 0))],
        out_specs=[row(d), row(d)],
        out_shape=[jax.ShapeDtypeStruct((t, d), F32), jax.ShapeDtypeStruct((t, d), BF16)],
        compiler_params=_cparams("parallel"),
        name="merge",
    )(att, yn, pf, pf, bg, bg, h, wa, ws, wo, lg.reshape(1, d), lb.reshape(1, d))


def _ffn_kernel(hb_ref, h_ref, w1_ref, w2_ref, lg_ref, lb_ref, of_ref, ob_ref, *, nsplit):
    hb = hb_ref[...]
    d_ff = w1_ref.shape[1]
    step = d_ff // nsplit
    acc = DN_ALPHA * h_ref[...]
    for j in range(nsplit):
        a = jnp.dot(hb, w1_ref[:, j * step:(j + 1) * step], preferred_element_type=F32)
        a = jnp.square(jnp.maximum(a, 0.0)).astype(BF16)
        acc = acc + jnp.dot(a, w2_ref[j * step:(j + 1) * step, :], preferred_element_type=F32)
    y = _ln_rows(acc, lg_ref[...], lb_ref[...])
    of_ref[...] = y
    ob_ref[...] = y.astype(BF16)


def ffn_call(hb, h, w1, w2, lg, lb, tm=256, nsplit=4):
    t, d = h.shape
    row = lambda: pl.BlockSpec((tm, d), lambda i: (i, 0))
    const = lambda a: pl.BlockSpec(a.shape, lambda i: (0,) * a.ndim)
    vec = pl.BlockSpec((1, d), lambda i: (0, 0))
    return pl.pallas_call(
        functools.partial(_ffn_kernel, nsplit=nsplit),
        grid=(t // tm,),
        in_specs=[row(), row(), const(w1), const(w2), vec, vec],
        out_specs=[row(), row()],
        out_shape=[jax.ShapeDtypeStruct((t, d), F32), jax.ShapeDtypeStruct((t, d), BF16)],
        compiler_params=_cparams("parallel"),
        name="ffn",
    )(hb, h, w1, w2, lg.reshape(1, d), lb.reshape(1, d))


def _split_w_in(w):
    o = 3 * ATTN_W
    conv_ch = D_INNER + 2 * SSD_GROUPS * D_STATE
    w_qkv = w[:, :o]
    w_z = w[:, o:o + D_INNER]
    w_xbc = w[:, o + D_INNER:o + D_INNER + conv_ch]
    w_dt = w[:, o + D_INNER + conv_ch:o + D_INNER + conv_ch + 2 * SSD_HEADS]
    w_gate = w[:, o + D_INNER + conv_ch + 2 * SSD_HEADS:]
    w_dt = w_dt.reshape(-1, 2, SSD_GROUPS, HEADS_PER_GROUP).transpose(0, 2, 1, 3).reshape(-1, 2 * SSD_HEADS)
    w_dt = jnp.pad(w_dt, ((0, 0), (0, LANES - 2 * SSD_HEADS)))
    return (w_qkv.astype(BF16), jnp.concatenate([w_xbc, w_z, w_gate], axis=1).astype(BF16), w_dt.astype(BF16))


def _group_rows(p):
    p = p.astype(F32).reshape(2, SSD_GROUPS, HEADS_PER_GROUP).transpose(1, 0, 2).reshape(SSD_GROUPS, 2 * HEADS_PER_GROUP)
    return jnp.broadcast_to(p[:, :, None], (SSD_GROUPS, 2 * HEADS_PER_GROUP, LANES))


def _mixer_layer(h, hb, bsz, s, w_in, conv_w, conv_b, a_log, dt_bias, d_skip, ssd_norm_w, rpb,
                 w_attn_br, w_ssd_br, b_gate, w_o, ln_g, ln_b):
    t = bsz * s
    w_qkv, w_f, w_dt = _split_w_in(w_in)
    qkv = matmul_call(hb, w_qkv, BF16, 1024, 1024, "proj_qkv")
    pf = matmul_call(hb, w_f, F32, 1024, 1024, "proj_f32")
    dt_raw = matmul_call(hb, w_dt, F32, 1024, LANES, "proj_dt")

    att = natten_call(qkv.reshape(bsz, s, 3 * ATTN_W), build_bias_table(rpb, s // GRID_W))

    dtr = dt_raw[:, :2 * SSD_HEADS].reshape(bsz, s, SSD_GROUPS, 2 * HEADS_PER_GROUP).transpose(0, 2, 3, 1)
    pa = _group_rows(-jnp.exp(a_log.astype(F32)))
    pb = _group_rows(dt_bias)
    dsk = jnp.repeat(d_skip.astype(F32), SSD_HEAD_DIM).reshape(1, D_INNER)
    yn = ssd_call(pf.reshape(bsz, s, -1), dtr, conv_w.astype(F32), conv_b.astype(F32).reshape(1, -1), pa, pb, dsk,
                  ssd_norm_w.astype(F32).reshape(1, D_INNER))

    return merge_call(att.reshape(t, ATTN_W), yn.reshape(t, D_INNER), pf, b_gate.astype(F32), h,
                      w_attn_br.astype(BF16), w_ssd_br.astype(BF16), w_o.astype(BF16), ln_g, ln_b)


def kernel(x, ln0_g, ln0_b, w_in, conv_w, conv_b, a_log, dt_bias, d_skip, ssd_norm_w, rpb, w_attn_br, w_ssd_br,
           b_gate, w_o, ln1_g, ln1_b, w_ff1, w_ff2, ln2_g, ln2_b):
    bsz, s, d = x.shape
    h, hb = layer_norm_call(x.reshape(bsz * s, d), ln0_g, ln0_b)
    for l in range(w_in.shape[0]):
        h, hb = _mixer_layer(h, hb, bsz, s, w_in[l], conv_w[l], conv_b[l], a_log[l], dt_bias[l], d_skip[l],
                             ssd_norm_w[l], rpb[l], w_attn_br[l], w_ssd_br[l], b_gate[l], w_o[l], ln1_g[l], ln1_b[l])
        h, hb = ffn_call(hb, h, w_ff1[l].astype(BF16), w_ff2[l].astype(BF16), ln2_g[l], ln2_b[l])
    return h.reshape(bsz, s, d)
```

```python
import functools
import math

import jax
import jax.numpy as jnp
import numpy as np
from jax import lax
from jax.experimental import pallas as pl
from jax.experimental.pallas import tpu as pltpu

D_MODEL = 1024
DEPTH = 2
GRID_W = 64
NA_HEADS = 16
NA_HEAD_DIM = 64
ATTN_W = NA_HEADS * NA_HEAD_DIM
WIN_R = 8
WIN_C = 16
D_INNER = 2048
SSD_HEAD_DIM = 64
SSD_HEADS = D_INNER // SSD_HEAD_DIM
SSD_GROUPS = 8
HEADS_PER_GROUP = SSD_HEADS // SSD_GROUPS
D_STATE = 128
CONV_K = 5
CHUNK = 128
GROUP_W = D_INNER // SSD_GROUPS
D_FF = 4 * D_MODEL
DN_ALPHA = (2 * DEPTH) ** 0.25
LN_EPS = 1e-5
LOG2E = math.log2(math.e)
RMS_EPS = 1e-5

V7X_VMEM_BYTES = 64 * 1024 * 1024
VMEM_LIMIT = 56 * 1024 * 1024
LANES = 128
NEG_BIG = -1e30

F32 = jnp.float32
BF16 = jnp.bfloat16


def _cparams(*sem):
    return pltpu.CompilerParams(dimension_semantics=sem, vmem_limit_bytes=VMEM_LIMIT)


def _ln_rows(x, g, b):
    mu = jnp.mean(x, axis=-1, keepdims=True)
    xc = x - mu
    var = jnp.mean(xc * xc, axis=-1, keepdims=True)
    return xc * lax.rsqrt(var + LN_EPS) * g + b


def _ln_kernel(x_ref, g_ref, b_ref, of_ref, ob_ref):
    y = _ln_rows(x_ref[...], g_ref[...], b_ref[...])
    of_ref[...] = y
    ob_ref[...] = y.astype(BF16)


def layer_norm_call(x, g, b, tm=512):
    t, d = x.shape
    return pl.pallas_call(
        _ln_kernel,
        grid=(t // tm,),
        in_specs=[pl.BlockSpec((tm, d), lambda i: (i, 0)),
                  pl.BlockSpec((1, d), lambda i: (0, 0)),
                  pl.BlockSpec((1, d), lambda i: (0, 0))],
        out_specs=[pl.BlockSpec((tm, d), lambda i: (i, 0)),
                   pl.BlockSpec((tm, d), lambda i: (i, 0))],
        out_shape=[jax.ShapeDtypeStruct((t, d), F32), jax.ShapeDtypeStruct((t, d), BF16)],
        compiler_params=_cparams("parallel"),
        name="ln0",
    )(x, g.reshape(1, d), b.reshape(1, d))


def _mm_kernel(x_ref, w_ref, o_ref):
    o_ref[...] = jnp.dot(x_ref[...], w_ref[...], preferred_element_type=F32).astype(o_ref.dtype)


def matmul_call(x, w, out_dtype, tm, tn, name):
    t, k = x.shape
    n = w.shape[1]
    return pl.pallas_call(
        _mm_kernel,
        grid=(n // tn, t // tm),
        in_specs=[pl.BlockSpec((tm, k), lambda j, i: (i, 0)),
                  pl.BlockSpec((k, tn), lambda j, i: (0, j))],
        out_specs=pl.BlockSpec((tm, tn), lambda j, i: (i, j)),
        out_shape=jax.ShapeDtypeStruct((t, n), out_dtype),
        compiler_params=_cparams("parallel", "parallel"),
        name=name,
    )(x, w)


Q_GROUP = 4
K_SLAB = Q_GROUP + WIN_R


def _slab_start(j, rows):
    return jnp.clip(Q_GROUP * j - WIN_R // 2, 0, rows - K_SLAB)


def _natten_kernel(q_ref, k_ref, v_ref, bias_ref, o_ref, *, rows):
    ngroups = rows // Q_GROUP
    nq = Q_GROUP * GRID_W
    nk = K_SLAB * GRID_W
    lane = lax.broadcasted_iota(jnp.int32, (nq, LANES), 1)
    first = lane < NA_HEAD_DIM

    def group_body(j, carry):
        variant = jnp.minimum(j, 1) + jnp.maximum(j - (ngroups - 2), 0)
        qs = pl.multiple_of(j * nq, nq)
        ks = pl.multiple_of(_slab_start(j, rows) * GRID_W, GRID_W)
        q = (q_ref[pl.ds(qs, nq), :].astype(F32) * (NA_HEAD_DIM ** -0.5 * LOG2E)).astype(BF16)
        kw = k_ref[pl.ds(ks, nk), :]
        vw = v_ref[pl.ds(ks, nk), :]
        outs = []
        for hh in range(2):
            keep = first if hh == 0 else jnp.logical_not(first)
            qh = jnp.where(keep, q, jnp.zeros_like(q))
            s = lax.dot_general(qh, kw, (((1,), (1,)), ((), ())), preferred_element_type=F32)
            s = s + bias_ref[variant, hh]
            m = jnp.max(s, axis=-1, keepdims=True)
            p = jnp.exp2(s - m)
            l = jnp.sum(p, axis=-1, keepdims=True)
            o = jnp.dot(p.astype(BF16), vw, preferred_element_type=F32)
            outs.append(o / l)
        o_ref[pl.ds(qs, nq), :] = jnp.where(first, outs[0], outs[1]).astype(o_ref.dtype)
        return carry

    lax.fori_loop(0, ngroups, group_body, 0, unroll=2)


def natten_call(qkv, bias_tab):
    bsz, s, _ = qkv.shape
    rows = s // GRID_W
    assert rows % Q_GROUP == 0 and rows >= K_SLAB + Q_GROUP
    npairs = NA_HEADS // 2
    blk = lambda off: pl.BlockSpec((None, s, LANES), lambda hp, b: (b, 0, off + hp))
    return pl.pallas_call(
        functools.partial(_natten_kernel, rows=rows),
        grid=(npairs, bsz),
        in_specs=[blk(0), blk(npairs), blk(2 * npairs),
                  pl.BlockSpec((3, 2) + bias_tab.shape[2:], lambda hp, b: (0, hp, 0, 0))],
        out_specs=pl.BlockSpec((None, s, LANES), lambda hp, b: (b, 0, hp)),
        out_shape=jax.ShapeDtypeStruct((bsz, s, ATTN_W), BF16),
        compiler_params=_cparams("parallel", "parallel"),
        name="natten",
    )(qkv, qkv, qkv, bias_tab)


def build_bias_table(rpb, rows):
    ngroups = rows // Q_GROUP
    nrow_off = 2 * WIN_R - 1
    pats = []
    for j in range(ngroups):
        ws = min(max(Q_GROUP * j - WIN_R // 2, 0), rows - K_SLAB)
        pat = -np.ones((Q_GROUP, K_SLAB), np.int64)
        for a in range(Q_GROUP):
            r = Q_GROUP * j + a
            r0 = min(max(r - WIN_R // 2, 0), rows - WIN_R)
            for i in range(K_SLAB):
                if r0 <= ws + i < r0 + WIN_R:
                    pat[a, i] = ws + i - r + WIN_R - 1
        pats.append(pat)
    assert all(np.array_equal(pats[1], p) for p in pats[1:-1])
    w = np.arange(GRID_W)[:, None]
    kc = np.arange(GRID_W)[None, :]
    c0 = np.clip(w - WIN_C // 2, 0, GRID_W - WIN_C)
    col_ok = (kc >= c0) & (kc < c0 + WIN_C)
    col_sel = np.zeros((GRID_W, GRID_W, 2 * WIN_C - 1), np.float32)
    col_sel[w, kc, np.clip(kc - w + WIN_C - 1, 0, 2 * WIN_C - 2)] = col_ok
    blocks = jnp.einsum("hrd,wkd->hrwk", rpb.astype(F32) * LOG2E, col_sel, precision=lax.Precision.HIGHEST)
    blocks = jnp.where(col_ok[None, None], blocks, NEG_BIG)
    blocks = jnp.concatenate([blocks, jnp.full_like(blocks[:, :1], NEG_BIG)], axis=1)
    variants = []
    for pat in (pats[0], pats[1], pats[-1]):
        idx = np.where(pat < 0, nrow_off, pat)
        rows_a = [jnp.concatenate([blocks[:, idx[a, i]] for i in range(K_SLAB)], axis=-1) for a in range(Q_GROUP)]
        variants.append(jnp.concatenate(rows_a, axis=-2))
    return jnp.stack(variants)


def _silu(x):
    return x * jax.nn.sigmoid(x)


def _softplus(x):
    return jnp.maximum(x, 0.0) + jnp.log1p(jnp.exp(-jnp.abs(x)))


def _chunk_scan(x, reverse):
    width = x.shape[1]
    lane = lax.broadcasted_iota(jnp.int32, x.shape, 1) & (CHUNK - 1)
    sh = 1
    while sh < CHUNK:
        if reverse:
            x = x + jnp.where(lane < CHUNK - sh, pltpu.roll(x, width - sh, 1), 0.0)
        else:
            x = x + jnp.where(lane >= sh, pltpu.roll(x, sh, 1), 0.0)
        sh *= 2
    return x


def _expand_heads_row(rows):
    lane = lax.broadcasted_iota(jnp.int32, (1, LANES), 1)
    first = lane < SSD_HEAD_DIM
    return jnp.concatenate([jnp.where(first, rows[0], rows[1]), jnp.where(first, rows[2], rows[3])], axis=1)


def _ssd_kernel(xs_ref, bm_ref, cm_ref, z_ref, dtr_ref, cwx_ref, cwb_ref, cwc_ref, cbx_ref, cbb_ref, cbc_ref,
                pa_ref, pb_ref, dsk_ref, nw_ref, o_ref,
                pad_x, pad_b, pad_c, xbd_s, bc_s, cc_s, bt_s, cb_s, rows_s, cols_s, y_s, st_s,
                *, seq):
    nchunk = seq // CHUNK
    halo = 8
    nh = HEADS_PER_GROUP
    lane256 = lax.broadcasted_iota(jnp.int32, (CHUNK, GROUP_W), 1)
    head_masks = [(lane256 >= SSD_HEAD_DIM * r) & (lane256 < SSD_HEAD_DIM * (r + 1)) for r in range(nh)]

    def conv_into(src_ref, pad_ref, w_ref, b_ref, store):
        width = src_ref.shape[-1]
        pad_ref[pl.ds(0, halo), :] = jnp.zeros((halo, width), F32)
        pad_ref[pl.ds(halo + seq, halo), :] = jnp.zeros((halo, width), F32)
        pad_ref[pl.ds(halo, seq), :] = src_ref[...]
        span = CHUNK + 2 * halo

        for c in range(nchunk):
            t0 = c * CHUNK
            acc = jnp.broadcast_to(b_ref[...], (CHUNK, width))
            for k in range(CONV_K):
                d = k - CONV_K // 2
                acc = acc + pad_ref[pl.ds(t0 + (halo + d), CHUNK), :] * w_ref[k:k + 1, :]
            store(c, t0, _silu(acc))

    def store_x(c, t0, v):
        y_s[pl.ds(t0, CHUNK), :] = v * dsk_ref[...]
        vb = v.astype(BF16)
        base = c * (nh * CHUNK)
        for r in range(nh):
            xbd_s[pl.ds(base + r * CHUNK, CHUNK), :] = jnp.where(head_masks[r], vb, jnp.zeros_like(vb))

    def store_b(c, t0, v):
        bc_s[pl.ds(t0, CHUNK), :] = v.astype(BF16)
        bt_s[:, pl.ds(t0, CHUNK)] = v.T.astype(BF16)

    def store_c(c, t0, v):
        cc_s[pl.ds(t0, CHUNK), :] = v.astype(BF16)

    conv_into(bm_ref, pad_b, cwb_ref, cbb_ref, store_b)
    conv_into(cm_ref, pad_c, cwc_ref, cbc_ref, store_c)
    conv_into(xs_ref, pad_x, cwx_ref, cbx_ref, store_x)

    a_rows = jnp.concatenate([pa_ref[...]] * nchunk, axis=1)
    bias_rows = jnp.concatenate([pb_ref[...]] * nchunk, axis=1)
    is_bwd = lax.broadcasted_iota(jnp.int32, (2 * nh, seq), 0) >= nh
    dt = _softplus(dtr_ref[...] + bias_rows)
    adt = dt * a_rows
    pre = _chunk_scan(adt, reverse=False)
    suf = _chunk_scan(adt, reverse=True)
    cs = jnp.where(is_bwd, suf, pre)
    total = pre + suf - adt
    rows_s[0] = (cs - jnp.log(dt)) * LOG2E
    rows_s[1] = jnp.exp(total)
    rows_s[2] = dt * jnp.exp(total - cs)
    rows_s[3] = cs * LOG2E

    def prep(c, carry):
        t0 = pl.multiple_of(c * CHUNK, CHUNK)
        stack = jnp.concatenate([rows_s[3, :, pl.ds(t0, CHUNK)], jnp.zeros((CHUNK - 2 * nh, CHUNK), F32)], axis=0)
        cols_s[pl.ds(t0, CHUNK), :] = stack.T
        cb_s[pl.ds(t0, CHUNK), :] = lax.dot_general(cc_s[pl.ds(t0, CHUNK), :], bc_s[pl.ds(t0, CHUNK), :],
                                                    (((1,), (1,)), ((), ())), preferred_element_type=F32)
        return carry

    lax.fori_loop(0, nchunk, prep, 0, unroll=4)

    row_i = lax.broadcasted_iota(jnp.int32, (CHUNK, CHUNK), 0)
    col_i = lax.broadcasted_iota(jnp.int32, (CHUNK, CHUNK), 1)
    st_s[...] = jnp.zeros(st_s.shape, F32)

    def scan_step(direction, c):
        causal = (row_i >= col_i) if direction == 0 else (row_i <= col_i)
        t0 = pl.multiple_of(c * CHUNK, CHUNK)
        ck = cc_s[pl.ds(t0, CHUNK), :]
        cb = cb_s[pl.ds(t0, CHUNK), :]
        cols = cols_s[pl.ds(t0, CHUNK), :]
        bt = bt_s[:, pl.ds(t0, CHUNK)]
        src = rows_s[0, :, pl.ds(t0, CHUNK)]
        decay = rows_s[1, :, pl.ds(t0, CHUNK)]
        w_state = rows_s[2, :, pl.ds(t0, CHUNK)].astype(BF16)
        j0 = direction * nh
        ms, bts, w_out = [], [], []
        for r in range(nh):
            cs_col = jnp.broadcast_to(cols[:, j0 + r:j0 + r + 1], (CHUNK, CHUNK))
            seg = cs_col - src[j0 + r:j0 + r + 1, :]
            ms.append((cb * jnp.exp2(jnp.where(causal, seg, NEG_BIG))).astype(BF16))
            bts.append(bt * w_state[j0 + r:j0 + r + 1, :])
            w_out.append(jnp.exp2(cs_col))
        lhs = jnp.concatenate([jnp.concatenate(ms, axis=1), jnp.concatenate(bts, axis=1)], axis=0)
        xbd = xbd_s[pl.ds(pl.multiple_of(c * (nh * CHUNK), nh * CHUNK), nh * CHUNK), :]
        both = jnp.dot(lhs, xbd, preferred_element_type=F32)
        first = col_i < SSD_HEAD_DIM
        w_out_x = jnp.concatenate([jnp.where(first, w_out[0], w_out[1]), jnp.where(first, w_out[2], w_out[3])], axis=1)
        prev = st_s[direction]
        y = both[:CHUNK] + jnp.dot(ck, prev.astype(BF16), preferred_element_type=F32) * w_out_x
        decay_x = _expand_heads_row([decay[j0 + r:j0 + r + 1, :] for r in range(nh)])
        st_s[direction] = prev * decay_x + both[CHUNK:]
        y_s[pl.ds(t0, CHUNK), :] = y_s[pl.ds(t0, CHUNK), :] + y

    def scan_body(i, carry):
        scan_step(0, i)
        scan_step(1, nchunk - 1 - i)
        return carry

    lax.fori_loop(0, nchunk, scan_body, 0, unroll=2)

    def fin(c, carry):
        t0 = pl.multiple_of(c * CHUNK, CHUNK)
        y = y_s[pl.ds(t0, CHUNK), :] * _silu(z_ref[pl.ds(t0, CHUNK), :])
        y = y * lax.rsqrt(jnp.mean(y * y, axis=-1, keepdims=True) + RMS_EPS)
        o_ref[pl.ds(t0, CHUNK), :] = (y * nw_ref[...]).astype(o_ref.dtype)
        return carry

    lax.fori_loop(0, nchunk, fin, 0, unroll=2)


def ssd_call(pf, dtr, cw, cb, pa, pb, dsk, nw):
    bsz, s, _ = pf.shape
    g_ = SSD_GROUPS
    nh = HEADS_PER_GROUP
    xo, bo, co, zo = 0, D_INNER // D_STATE, (D_INNER + g_ * D_STATE) // D_STATE, (D_INNER + 2 * g_ * D_STATE) // GROUP_W
    cwx, cwb, cwc = cw[:, :D_INNER], cw[:, D_INNER:D_INNER + g_ * D_STATE], cw[:, D_INNER + g_ * D_STATE:]
    cbx, cbb, cbc = cb[:, :D_INNER], cb[:, D_INNER:D_INNER + g_ * D_STATE], cb[:, D_INNER + g_ * D_STATE:]
    wide = lambda off: pl.BlockSpec((None, s, GROUP_W), lambda b, g: (b, 0, off + g))
    narrow = lambda off: pl.BlockSpec((None, s, D_STATE), lambda b, g: (b, 0, off + g))
    pw = lambda rows, width: pl.BlockSpec((rows, width), lambda b, g: (0, g))
    return pl.pallas_call(
        functools.partial(_ssd_kernel, seq=s),
        grid=(bsz, g_),
        in_specs=[wide(xo), narrow(bo), narrow(co), wide(zo),
                  pl.BlockSpec((None, None, 2 * nh, s), lambda b, g: (b, g, 0, 0)),
                  pw(CONV_K, GROUP_W), pw(CONV_K, D_STATE), pw(CONV_K, D_STATE),
                  pw(1, GROUP_W), pw(1, D_STATE), pw(1, D_STATE),
                  pl.BlockSpec((None, 2 * nh, LANES), lambda b, g: (g, 0, 0)),
                  pl.BlockSpec((None, 2 * nh, LANES), lambda b, g: (g, 0, 0)),
                  pw(1, GROUP_W), pw(1, GROUP_W)],
        out_specs=pl.BlockSpec((None, s, GROUP_W), lambda b, g: (b, 0, g)),
        out_shape=jax.ShapeDtypeStruct((bsz, s, D_INNER), BF16),
        scratch_shapes=[pltpu.VMEM((s + 16, GROUP_W), F32),
                        pltpu.VMEM((s + 16, D_STATE), F32),
                        pltpu.VMEM((s + 16, D_STATE), F32),
                        pltpu.VMEM((nh * s, GROUP_W), BF16),
                        pltpu.VMEM((s, D_STATE), BF16),
                        pltpu.VMEM((s, D_STATE), BF16),
                        pltpu.VMEM((D_STATE, s), BF16),
                        pltpu.VMEM((s, CHUNK), F32),
                        pltpu.VMEM((4, 2 * nh, s), F32),
                        pltpu.VMEM((s, LANES), F32),
                        pltpu.VMEM((s, GROUP_W), F32),
                        pltpu.VMEM((2, D_STATE, GROUP_W), F32)],
        compiler_params=_cparams("parallel", "parallel"),
        name="ssd",
    )(pf, pf, pf, pf, dtr, cwx, cwb, cwc, cbx, cbb, cbc, pa, pb, dsk, nw)


def _merge_kernel(att_ref, yn_ref, g0_ref, g1_ref, bg0_ref, bg1_ref, h_ref, wa_ref, ws_ref, wo_ref,
                  lg_ref, lb_ref, of_ref, ob_ref):
    ya = jnp.dot(att_ref[...], wa_ref[...], preferred_element_type=F32)
    ys = jnp.dot(yn_ref[...], ws_ref[...], preferred_element_type=F32)
    merged = jax.nn.sigmoid(g0_ref[...] + bg0_ref[...]) * ya + jax.nn.sigmoid(g1_ref[...] + bg1_ref[...]) * ys
    mix = jnp.dot(merged.astype(BF16), wo_ref[...], preferred_element_type=F32)
    y = _ln_rows(DN_ALPHA * h_ref[...] + mix, lg_ref[...], lb_ref[...])
    of_ref[...] = y
    ob_ref[...] = y.astype(BF16)


def merge_call(att, yn, pf, b_gate, h, wa, ws, wo, lg, lb, tm=256):
    t, d = h.shape
    gate_off = (pf.shape[1] - 2 * d) // d
    row = lambda width: pl.BlockSpec((tm, width), lambda i: (i, 0))
    const = lambda a: pl.BlockSpec(a.shape, lambda i: (0,) * a.ndim)
    bg = b_gate.reshape(1, 2 * d)
    return pl.pallas_call(
        _merge_kernel,
        grid=(t // tm,),
        in_specs=[row(d), row(yn.shape[1]),
                  pl.BlockSpec((tm, d), lambda i: (i, gate_off)),
                  pl.BlockSpec((tm, d), lambda i: (i, gate_off + 1)),
                  pl.BlockSpec((1, d), lambda i: (0, 0)), pl.BlockSpec((1, d), lambda i: (0, 1)),
                  row(d), const(wa), const(ws), const(wo),
                  pl.BlockSpec((1, d), lambda i: (0, 0)), pl.BlockSpec((1, d), lambda i: (0, 0))],
        out_specs=[row(d), row(d)],
        out_shape=[jax.ShapeDtypeStruct((t, d), F32), jax.ShapeDtypeStruct((t, d), BF16)],
        compiler_params=_cparams("parallel"),
        name="merge",
    )(att, yn, pf, pf, bg, bg, h, wa, ws, wo, lg.reshape(1, d), lb.reshape(1, d))


def _ffn_kernel(hb_ref, h_ref, w1_ref, w2_ref, lg_ref, lb_ref, of_ref, ob_ref, *, nsplit):
    hb = hb_ref[...]
    d_ff = w1_ref.shape[1]
    step = d_ff // nsplit
    acc = DN_ALPHA * h_ref[...]
    for j in range(nsplit):
        a = jnp.dot(hb, w1_ref[:, j * step:(j + 1) * step], preferred_element_type=F32)
        a = jnp.square(jnp.maximum(a, 0.0)).astype(BF16)
        acc = acc + jnp.dot(a, w2_ref[j * step:(j + 1) * step, :], preferred_element_type=F32)
    y = _ln_rows(acc, lg_ref[...], lb_ref[...])
    of_ref[...] = y
    ob_ref[...] = y.astype(BF16)


def ffn_call(hb, h, w1, w2, lg, lb, tm=256, nsplit=4):
    t, d = h.shape
    row = lambda: pl.BlockSpec((tm, d), lambda i: (i, 0))
    const = lambda a: pl.BlockSpec(a.shape, lambda i: (0,) * a.ndim)
    vec = pl.BlockSpec((1, d), lambda i: (0, 0))
    return pl.pallas_call(
        functools.partial(_ffn_kernel, nsplit=nsplit),
        grid=(t // tm,),
        in_specs=[row(), row(), const(w1), const(w2), vec, vec],
        out_specs=[row(), row()],
        out_shape=[jax.ShapeDtypeStruct((t, d), F32), jax.ShapeDtypeStruct((t, d), BF16)],
        compiler_params=_cparams("parallel"),
        name="ffn",
    )(hb, h, w1, w2, lg.reshape(1, d), lb.reshape(1, d))


def _split_w_in(w):
    o = 3 * ATTN_W
    conv_ch = D_INNER + 2 * SSD_GROUPS * D_STATE
    w_qkv = w[:, :o]
    w_z = w[:, o:o + D_INNER]
    w_xbc = w[:, o + D_INNER:o + D_INNER + conv_ch]
    w_dt = w[:, o + D_INNER + conv_ch:o + D_INNER + conv_ch + 2 * SSD_HEADS]
    w_gate = w[:, o + D_INNER + conv_ch + 2 * SSD_HEADS:]
    w_dt = w_dt.reshape(-1, 2, SSD_GROUPS, HEADS_PER_GROUP).transpose(0, 2, 1, 3).reshape(-1, 2 * SSD_HEADS)
    w_dt = jnp.pad(w_dt, ((0, 0), (0, LANES - 2 * SSD_HEADS)))
    return (w_qkv.astype(BF16), jnp.concatenate([w_xbc, w_z, w_gate], axis=1).astype(BF16), w_dt.astype(BF16))


def _group_rows(p):
    p = p.astype(F32).reshape(2, SSD_GROUPS, HEADS_PER_GROUP).transpose(1, 0, 2).reshape(SSD_GROUPS, 2 * HEADS_PER_GROUP)
    return jnp.broadcast_to(p[:, :, None], (SSD_GROUPS, 2 * HEADS_PER_GROUP, LANES))


def _mixer_layer(h, hb, bsz, s, w_in, conv_w, conv_b, a_log, dt_bias, d_skip, ssd_norm_w, rpb,
                 w_attn_br, w_ssd_br, b_gate, w_o, ln_g, ln_b):
    t = bsz * s
    w_qkv, w_f, w_dt = _split_w_in(w_in)
    qkv = matmul_call(hb, w_qkv, BF16, 1024, 1024, "proj_qkv")
    pf = matmul_call(hb, w_f, F32, 1024, 1024, "proj_f32")
    dt_raw = matmul_call(hb, w_dt, F32, 1024, LANES, "proj_dt")

    att = natten_call(qkv.reshape(bsz, s, 3 * ATTN_W), build_bias_table(rpb, s // GRID_W))

    dtr = dt_raw[:, :2 * SSD_HEADS].reshape(bsz, s, SSD_GROUPS, 2 * HEADS_PER_GROUP).transpose(0, 2, 3, 1)
    pa = _group_rows(-jnp.exp(a_log.astype(F32)))
    pb = _group_rows(dt_bias)
    dsk = jnp.repeat(d_skip.astype(F32), SSD_HEAD_DIM).reshape(1, D_INNER)
    yn = ssd_call(pf.reshape(bsz, s, -1), dtr, conv_w.astype(F32), conv_b.astype(F32).reshape(1, -1), pa, pb, dsk,
                  ssd_norm_w.astype(F32).reshape(1, D_INNER))

    return merge_call(att.reshape(t, ATTN_W), yn.reshape(t, D_INNER), pf, b_gate.astype(F32), h,
                      w_attn_br.astype(BF16), w_ssd_br.astype(BF16), w_o.astype(BF16), ln_g, ln_b)


def kernel(x, ln0_g, ln0_b, w_in, conv_w, conv_b, a_log, dt_bias, d_skip, ssd_norm_w, rpb, w_attn_br, w_ssd_br,
           b_gate, w_o, ln1_g, ln1_b, w_ff1, w_ff2, ln2_g, ln2_b):
    bsz, s, d = x.shape
    h, hb = layer_norm_call(x.reshape(bsz * s, d), ln0_g, ln0_b)
    for l in range(w_in.shape[0]):
        h, hb = _mixer_layer(h, hb, bsz, s, w_in[l], conv_w[l], conv_b[l], a_log[l], dt_bias[l], d_skip[l],
                             ssd_norm_w[l], rpb[l], w_attn_br[l], w_ssd_br[l], b_gate[l], w_o[l], ln1_g[l], ln1_b[l])
        h, hb = ffn_call(hb, h, w_ff1[l].astype(BF16), w_ff2[l].astype(BF16), ln2_g[l], ln2_b[l])
    return h.reshape(bsz, s, d)
```

```python
import functools
import math

import jax
import jax.numpy as jnp
import numpy as np
from jax import lax
from jax.experimental import pallas as pl
from jax.experimental.pallas import tpu as pltpu

D_MODEL = 1024
DEPTH = 2
GRID_W = 64
NA_HEADS = 16
NA_HEAD_DIM = 64
ATTN_W = NA_HEADS * NA_HEAD_DIM
WIN_R = 8
WIN_C = 16
D_INNER = 2048
SSD_HEAD_DIM = 64
SSD_HEADS = D_INNER // SSD_HEAD_DIM
SSD_GROUPS = 8
HEADS_PER_GROUP = SSD_HEADS // SSD_GROUPS
D_STATE = 128
CONV_K = 5
CHUNK = 128
GROUP_W = D_INNER // SSD_GROUPS
D_FF = 4 * D_MODEL
DN_ALPHA = (2 * DEPTH) ** 0.25
LN_EPS = 1e-5
LOG2E = math.log2(math.e)
RMS_EPS = 1e-5

V7X_VMEM_BYTES = 64 * 1024 * 1024
VMEM_LIMIT = 56 * 1024 * 1024
LANES = 128
NEG_BIG = -1e30

F32 = jnp.float32
BF16 = jnp.bfloat16


def _cparams(*sem):
    return pltpu.CompilerParams(dimension_semantics=sem, vmem_limit_bytes=VMEM_LIMIT)


def _ln_rows(x, g, b):
    mu = jnp.mean(x, axis=-1, keepdims=True)
    xc = x - mu
    var = jnp.mean(xc * xc, axis=-1, keepdims=True)
    return xc * lax.rsqrt(var + LN_EPS) * g + b


def _ln_kernel(x_ref, g_ref, b_ref, of_ref, ob_ref):
    y = _ln_rows(x_ref[...], g_ref[...], b_ref[...])
    of_ref[...] = y
    ob_ref[...] = y.astype(BF16)


def layer_norm_call(x, g, b, tm=512):
    t, d = x.shape
    return pl.pallas_call(
        _ln_kernel,
        grid=(t // tm,),
        in_specs=[pl.BlockSpec((tm, d), lambda i: (i, 0)),
                  pl.BlockSpec((1, d), lambda i: (0, 0)),
                  pl.BlockSpec((1, d), lambda i: (0, 0))],
        out_specs=[pl.BlockSpec((tm, d), lambda i: (i, 0)),
                   pl.BlockSpec((tm, d), lambda i: (i, 0))],
        out_shape=[jax.ShapeDtypeStruct((t, d), F32), jax.ShapeDtypeStruct((t, d), BF16)],
        compiler_params=_cparams("parallel"),
        name="ln0",
    )(x, g.reshape(1, d), b.reshape(1, d))


def _mm_kernel(x_ref, w_ref, o_ref):
    o_ref[...] = jnp.dot(x_ref[...], w_ref[...], preferred_element_type=F32).astype(o_ref.dtype)


def matmul_call(x, w, out_dtype, tm, tn, name):
    t, k = x.shape
    n = w.shape[1]
    return pl.pallas_call(
        _mm_kernel,
        grid=(n // tn, t // tm),
        in_specs=[pl.BlockSpec((tm, k), lambda j, i: (i, 0)),
                  pl.BlockSpec((k, tn), lambda j, i: (0, j))],
        out_specs=pl.BlockSpec((tm, tn), lambda j, i: (i, j)),
        out_shape=jax.ShapeDtypeStruct((t, n), out_dtype),
        compiler_params=_cparams("parallel", "parallel"),
        name=name,
    )(x, w)


Q_GROUP = 4
K_SLAB = Q_GROUP + WIN_R


def _slab_start(j, rows):
    return jnp.clip(Q_GROUP * j - WIN_R // 2, 0, rows - K_SLAB)


def _group_patterns(rows):
    ngroups = rows // Q_GROUP
    outside = 2 * WIN_R - 1
    pats = []
    for j in range(ngroups):
        ws = min(max(Q_GROUP * j - WIN_R // 2, 0), rows - K_SLAB)
        pat = np.full((Q_GROUP, K_SLAB), outside, np.int64)
        for a in range(Q_GROUP):
            r = Q_GROUP * j + a
            r0 = min(max(r - WIN_R // 2, 0), rows - WIN_R)
            for i in range(K_SLAB):
                if r0 <= ws + i < r0 + WIN_R:
                    pat[a, i] = ws + i - r + WIN_R - 1
        pats.append(pat)
    assert all(np.array_equal(pats[1], p) for p in pats[1:-1])
    return [pats[0].tolist(), pats[1].tolist(), pats[-1].tolist()]


def _natten_kernel(q_ref, k_ref, v_ref, blk_ref, o_ref, bias_s, *, rows):
    ngroups = rows // Q_GROUP
    nq = Q_GROUP * GRID_W
    nk = K_SLAB * GRID_W
    lane = lax.broadcasted_iota(jnp.int32, (nq, LANES), 1)
    first = lane < NA_HEAD_DIM

    @pl.when(pl.program_id(1) == 0)
    def _():
        for v, pat in enumerate(_group_patterns(rows)):
            for hh in range(2):
                for a in range(Q_GROUP):
                    for i in range(0, K_SLAB, 2):
                        pair = jnp.concatenate([blk_ref[hh, pat[a][i]], blk_ref[hh, pat[a][i + 1]]], axis=1)
                        bias_s[v, hh, a * GRID_W:(a + 1) * GRID_W, i * GRID_W:(i + 2) * GRID_W] = pair

    def group_body(j, carry):
        variant = jnp.minimum(j, 1) + jnp.maximum(j - (ngroups - 2), 0)
        qs = pl.multiple_of(j * nq, nq)
        ks = pl.multiple_of(_slab_start(j, rows) * GRID_W, GRID_W)
        q = (q_ref[pl.ds(qs, nq), :].astype(F32) * (NA_HEAD_DIM ** -0.5 * LOG2E)).astype(BF16)
        kw = k_ref[pl.ds(ks, nk), :]
        vw = v_ref[pl.ds(ks, nk), :]
        outs = []
        for hh in range(2):
            keep = first if hh == 0 else jnp.logical_not(first)
            qh = jnp.where(keep, q, jnp.zeros_like(q))
            s = lax.dot_general(qh, kw, (((1,), (1,)), ((), ())), preferred_element_type=F32)
            s = s + bias_s[variant, hh]
            m = jnp.max(s, axis=-1, keepdims=True)
            p = jnp.exp2(s - m)
            l = jnp.sum(p, axis=-1, keepdims=True)
            o = jnp.dot(p.astype(BF16), vw, preferred_element_type=F32)
            outs.append(o / l)
        o_ref[pl.ds(qs, nq), :] = jnp.where(first, outs[0], outs[1]).astype(o_ref.dtype)
        return carry

    lax.fori_loop(0, ngroups, group_body, 0, unroll=2)


def natten_call(qkv, blocks):
    bsz, s, _ = qkv.shape
    rows = s // GRID_W
    assert rows % Q_GROUP == 0 and rows >= K_SLAB + Q_GROUP and K_SLAB % 2 == 0
    npairs = NA_HEADS // 2
    blk = lambda off: pl.BlockSpec((None, s, LANES), lambda hp, b: (b, 0, off + hp))
    return pl.pallas_call(
        functools.partial(_natten_kernel, rows=rows),
        grid=(npairs, bsz),
        in_specs=[blk(0), blk(npairs), blk(2 * npairs),
                  pl.BlockSpec((2,) + blocks.shape[1:], lambda hp, b: (hp, 0, 0, 0))],
        out_specs=pl.BlockSpec((None, s, LANES), lambda hp, b: (b, 0, hp)),
        out_shape=jax.ShapeDtypeStruct((bsz, s, ATTN_W), BF16),
        scratch_shapes=[pltpu.VMEM((3, 2, Q_GROUP * GRID_W, K_SLAB * GRID_W), F32)],
        compiler_params=_cparams("arbitrary", "arbitrary"),
        name="natten",
    )(qkv, qkv, qkv, blocks)


def build_bias_blocks(rpb):
    w = np.arange(GRID_W)[:, None]
    kc = np.arange(GRID_W)[None, :]
    c0 = np.clip(w - WIN_C // 2, 0, GRID_W - WIN_C)
    col_ok = (kc >= c0) & (kc < c0 + WIN_C)
    col_sel = np.zeros((GRID_W, GRID_W, 2 * WIN_C - 1), np.float32)
    col_sel[w, kc, np.clip(kc - w + WIN_C - 1, 0, 2 * WIN_C - 2)] = col_ok
    blocks = jnp.einsum("hrd,wkd->hrwk", rpb.astype(F32) * LOG2E, col_sel, precision=lax.Precision.HIGHEST)
    blocks = jnp.where(col_ok[None, None], blocks, NEG_BIG)
    return jnp.concatenate([blocks, jnp.full_like(blocks[:, :1], NEG_BIG)], axis=1)


def _silu(x):
    return x * jax.nn.sigmoid(x)


def _softplus(x):
    return jnp.maximum(x, 0.0) + jnp.log1p(jnp.exp(-jnp.abs(x)))


def _chunk_scan(x, reverse):
    width = x.shape[1]
    lane = lax.broadcasted_iota(jnp.int32, x.shape, 1) & (CHUNK - 1)
    sh = 1
    while sh < CHUNK:
        if reverse:
            x = x + jnp.where(lane < CHUNK - sh, pltpu.roll(x, width - sh, 1), 0.0)
        else:
            x = x + jnp.where(lane >= sh, pltpu.roll(x, sh, 1), 0.0)
        sh *= 2
    return x


def _expand_heads_row(rows):
    lane = lax.broadcasted_iota(jnp.int32, (1, LANES), 1)
    first = lane < SSD_HEAD_DIM
    return jnp.concatenate([jnp.where(first, rows[0], rows[1]), jnp.where(first, rows[2], rows[3])], axis=1)


def _ssd_kernel(xs_ref, bm_ref, cm_ref, z_ref, dtr_ref, cwx_ref, cwb_ref, cwc_ref, cbx_ref, cbb_ref, cbc_ref,
                pa_ref, pb_ref, dsk_ref, nw_ref, spread_ref, o_ref,
                pad_x, pad_b, pad_c, xbd_s, bc_s, cc_s, bt_s, cb_s, rows_s, csb_s, y_s, st_s,
                *, seq):
    nchunk = seq // CHUNK
    halo = 8
    nh = HEADS_PER_GROUP
    lane256 = lax.broadcasted_iota(jnp.int32, (CHUNK, GROUP_W), 1)
    head_masks = [(lane256 >= SSD_HEAD_DIM * r) & (lane256 < SSD_HEAD_DIM * (r + 1)) for r in range(nh)]

    a_rows = jnp.concatenate([pa_ref[...]] * nchunk, axis=1)
    bias_rows = jnp.concatenate([pb_ref[...]] * nchunk, axis=1)
    is_bwd = lax.broadcasted_iota(jnp.int32, (2 * nh, seq), 0) >= nh
    dt = _softplus(dtr_ref[...] + bias_rows)
    adt = dt * a_rows
    pre = _chunk_scan(adt, reverse=False)
    suf = _chunk_scan(adt, reverse=True)
    cs = jnp.where(is_bwd, suf, pre)
    total = pre + suf - adt
    rows_s[0] = (cs - jnp.log(dt)) * LOG2E
    rows_s[1] = jnp.exp(total)
    rows_s[2] = dt * jnp.exp(total - cs)
    cs2 = cs * LOG2E
    hi = cs2.astype(BF16).astype(F32)
    mid = (cs2 - hi).astype(BF16).astype(F32)
    rows_s[3] = hi
    rows_s[4] = mid
    rows_s[5] = ((cs2 - hi) - mid).astype(BF16).astype(F32)

    def pad_copy(src_ref, pad_ref):
        width = src_ref.shape[-1]
        pad_ref[pl.ds(0, halo), :] = jnp.zeros((halo, width), F32)
        pad_ref[pl.ds(halo + seq, halo), :] = jnp.zeros((halo, width), F32)
        pad_ref[pl.ds(halo, seq), :] = src_ref[...]

    def conv_chunk(pad_ref, w_ref, b_ref, t0):
        acc = jnp.broadcast_to(b_ref[...], (CHUNK, pad_ref.shape[-1]))
        for k in range(CONV_K):
            acc = acc + pad_ref[pl.ds(t0 + halo + k - CONV_K // 2, CHUNK), :] * w_ref[k:k + 1, :]
        return _silu(acc)

    pad_copy(bm_ref, pad_b)
    pad_copy(cm_ref, pad_c)
    pad_copy(xs_ref, pad_x)
    for c in range(nchunk):
        t0 = c * CHUNK
        vb = conv_chunk(pad_b, cwb_ref, cbb_ref, t0)
        vc = conv_chunk(pad_c, cwc_ref, cbc_ref, t0).astype(BF16)
        vx = conv_chunk(pad_x, cwx_ref, cbx_ref, t0)
        bc_s[pl.ds(t0, CHUNK), :] = vb.astype(BF16)
        bt_s[:, pl.ds(t0, CHUNK)] = vb.T.astype(BF16)
        cc_s[pl.ds(t0, CHUNK), :] = vc
        y_s[pl.ds(t0, CHUNK), :] = vx * dsk_ref[...]
        vxb = vx.astype(BF16)
        for r in range(nh):
            xbd_s[pl.ds((c * nh + r) * CHUNK, CHUNK), :] = jnp.where(head_masks[r], vxb, jnp.zeros_like(vxb))
        cb_s[pl.ds(t0, CHUNK), :] = lax.dot_general(vc, vb.astype(BF16), (((1,), (1,)), ((), ())),
                                                    preferred_element_type=F32)
        pieces = jnp.concatenate([rows_s[3, :, pl.ds(t0, CHUNK)], rows_s[4, :, pl.ds(t0, CHUNK)],
                                  rows_s[5, :, pl.ds(t0, CHUNK)], jnp.zeros((CHUNK - 6 * nh, CHUNK), F32)], axis=0)
        csb_s[pl.ds(t0, CHUNK), :] = jnp.dot(pieces.T.astype(BF16), spread_ref[...], preferred_element_type=F32)

    row_i = lax.broadcasted_iota(jnp.int32, (CHUNK, CHUNK), 0)
    col_i = lax.broadcasted_iota(jnp.int32, (CHUNK, CHUNK), 1)
    st_s[...] = jnp.zeros(st_s.shape, F32)

    def scan_step(direction, c):
        causal = (row_i >= col_i) if direction == 0 else (row_i <= col_i)
        t0 = pl.multiple_of(c * CHUNK, CHUNK)
        ck = cc_s[pl.ds(t0, CHUNK), :]
        cb = cb_s[pl.ds(t0, CHUNK), :]
        bt = bt_s[:, pl.ds(t0, CHUNK)]
        src = rows_s[0, :, pl.ds(t0, CHUNK)]
        decay = rows_s[1, :, pl.ds(t0, CHUNK)]
        w_state = rows_s[2, :, pl.ds(t0, CHUNK)].astype(BF16)
        j0 = direction * nh
        ms, bts, w_out = [], [], []
        for r in range(nh):
            cs_col = csb_s[pl.ds(t0, CHUNK), (j0 + r) * LANES:(j0 + r + 1) * LANES]
            seg = cs_col - src[j0 + r:j0 + r + 1, :]
            ms.append((cb * jnp.exp2(jnp.where(causal, seg, NEG_BIG))).astype(BF16))
            bts.append(bt * w_state[j0 + r:j0 + r + 1, :])
            w_out.append(jnp.exp2(cs_col))
        lhs = jnp.concatenate([jnp.concatenate(ms, axis=1), jnp.concatenate(bts, axis=1)], axis=0)
        xbd = xbd_s[pl.ds(pl.multiple_of(c * (nh * CHUNK), nh * CHUNK), nh * CHUNK), :]
        both = jnp.dot(lhs, xbd, preferred_element_type=F32)
        first = col_i < SSD_HEAD_DIM
        w_out_x = jnp.concatenate([jnp.where(first, w_out[0], w_out[1]), jnp.where(first, w_out[2], w_out[3])], axis=1)
        prev = st_s[direction]
        y = both[:CHUNK] + jnp.dot(ck, prev.astype(BF16), preferred_element_type=F32) * w_out_x
        decay_x = _expand_heads_row([decay[j0 + r:j0 + r + 1, :] for r in range(nh)])
        st_s[direction] = prev * decay_x + both[CHUNK:]
        y_s[pl.ds(t0, CHUNK), :] = y_s[pl.ds(t0, CHUNK), :] + y

    def scan_body(i, carry):
        scan_step(0, i)
        scan_step(1, nchunk - 1 - i)
        return carry

    lax.fori_loop(0, nchunk, scan_body, 0, unroll=4)

    def fin(c, carry):
        t0 = pl.multiple_of(c * CHUNK, CHUNK)
        y = y_s[pl.ds(t0, CHUNK), :] * _silu(z_ref[pl.ds(t0, CHUNK), :])
        y = y * lax.rsqrt(jnp.mean(y * y, axis=-1, keepdims=True) + RMS_EPS)
        o_ref[pl.ds(t0, CHUNK), :] = (y * nw_ref[...]).astype(o_ref.dtype)
        return carry

    lax.fori_loop(0, nchunk, fin, 0, unroll=4)


def ssd_call(pf, dtr, cw, cb, pa, pb, dsk, nw):
    bsz, s, _ = pf.shape
    g_ = SSD_GROUPS
    nh = HEADS_PER_GROUP
    xo, bo, co, zo = 0, D_INNER // D_STATE, (D_INNER + g_ * D_STATE) // D_STATE, (D_INNER + 2 * g_ * D_STATE) // GROUP_W
    cwx, cwb, cwc = cw[:, :D_INNER], cw[:, D_INNER:D_INNER + g_ * D_STATE], cw[:, D_INNER + g_ * D_STATE:]
    cbx, cbb, cbc = cb[:, :D_INNER], cb[:, D_INNER:D_INNER + g_ * D_STATE], cb[:, D_INNER + g_ * D_STATE:]
    wide = lambda off: pl.BlockSpec((None, s, GROUP_W), lambda b, g: (b, 0, off + g))
    narrow = lambda off: pl.BlockSpec((None, s, D_STATE), lambda b, g: (b, 0, off + g))
    pw = lambda rows, width: pl.BlockSpec((rows, width), lambda b, g: (0, g))
    spread = np.zeros((LANES, 2 * nh * LANES), np.float32)
    for p in range(3):
        for j in range(2 * nh):
            spread[2 * nh * p + j, j * LANES:(j + 1) * LANES] = 1.0
    spread = jnp.asarray(spread, BF16)
    return pl.pallas_call(
        functools.partial(_ssd_kernel, seq=s),
        grid=(bsz, g_),
        in_specs=[wide(xo), narrow(bo), narrow(co), wide(zo),
                  pl.BlockSpec((None, None, 2 * nh, s), lambda b, g: (b, g, 0, 0)),
                  pw(CONV_K, GROUP_W), pw(CONV_K, D_STATE), pw(CONV_K, D_STATE),
                  pw(1, GROUP_W), pw(1, D_STATE), pw(1, D_STATE),
                  pl.BlockSpec((None, 2 * nh, LANES), lambda b, g: (g, 0, 0)),
                  pl.BlockSpec((None, 2 * nh, LANES), lambda b, g: (g, 0, 0)),
                  pw(1, GROUP_W), pw(1, GROUP_W),
                  pl.BlockSpec(spread.shape, lambda b, g: (0, 0))],
        out_specs=pl.BlockSpec((None, s, GROUP_W), lambda b, g: (b, 0, g)),
        out_shape=jax.ShapeDtypeStruct((bsz, s, D_INNER), BF16),
        scratch_shapes=[pltpu.VMEM((s + 16, GROUP_W), F32),
                        pltpu.VMEM((s + 16, D_STATE), F32),
                        pltpu.VMEM((s + 16, D_STATE), F32),
                        pltpu.VMEM((nh * s, GROUP_W), BF16),
                        pltpu.VMEM((s, D_STATE), BF16),
                        pltpu.VMEM((s, D_STATE), BF16),
                        pltpu.VMEM((D_STATE, s), BF16),
                        pltpu.VMEM((s, CHUNK), F32),
                        pltpu.VMEM((6, 2 * nh, s), F32),
                        pltpu.VMEM((s, 2 * nh * LANES), F32),
                        pltpu.VMEM((s, GROUP_W), F32),
                        pltpu.VMEM((2, D_STATE, GROUP_W), F32)],
        compiler_params=_cparams("parallel", "parallel"),
        name="ssd",
    )(pf, pf, pf, pf, dtr, cwx, cwb, cwc, cbx, cbb, cbc, pa, pb, dsk, nw, spread)


def _merge_kernel(att_ref, yn_ref, g0_ref, g1_ref, bg0_ref, bg1_ref, h_ref, wa_ref, ws_ref, wo_ref,
                  lg_ref, lb_ref, of_ref, ob_ref):
    ya = jnp.dot(att_ref[...], wa_ref[...], preferred_element_type=F32)
    ys = jnp.dot(yn_ref[...], ws_ref[...], preferred_element_type=F32)
    merged = jax.nn.sigmoid(g0_ref[...] + bg0_ref[...]) * ya + jax.nn.sigmoid(g1_ref[...] + bg1_ref[...]) * ys
    mix = jnp.dot(merged.astype(BF16), wo_ref[...], preferred_element_type=F32)
    y = _ln_rows(DN_ALPHA * h_ref[...] + mix, lg_ref[...], lb_ref[...])
    of_ref[...] = y
    ob_ref[...] = y.astype(BF16)


def merge_call(att, yn, pf, b_gate, h, wa, ws, wo, lg, lb, tm=256):
    t, d = h.shape
    gate_off = (pf.shape[1] - 2 * d) // d
    row = lambda width: pl.BlockSpec((tm, width), lambda i: (i, 0))
    const = lambda a: pl.BlockSpec(a.shape, lambda i: (0,) * a.ndim)
    bg = b_gate.reshape(1, 2 * d)
    return pl.pallas_call(
        _merge_kernel,
        grid=(t // tm,),
        in_specs=[row(d), row(yn.shape[1]),
                  pl.BlockSpec((tm, d), lambda i: (i, gate_off)),
                  pl.BlockSpec((tm, d), lambda i: (i, gate_off + 1)),
                  pl.BlockSpec((1, d), lambda i: (0, 0)), pl.BlockSpec((1, d), lambda i: (0, 1)),
                  row(d), const(wa), const(ws), const(wo),
                  pl.BlockSpec((1, d), lambda i: (0, 0)), pl.BlockSpec((1, d), lambda i: (0, 0))],
        out_specs=[row(d), row(d)],
        out_shape=[jax.ShapeDtypeStruct((t, d), F32), jax.ShapeDtypeStruct((t, d), BF16)],
        compiler_params=_cparams("parallel"),
        name="merge",
    )(att, yn, pf, pf, bg, bg, h, wa, ws, wo, lg.reshape(1, d), lb.reshape(1, d))


def _ffn_kernel(hb_ref, h_ref, w1_ref, w2_ref, lg_ref, lb_ref, of_ref, ob_ref, *, nsplit):
    hb = hb_ref[...]
    d_ff = w1_ref.shape[1]
    step = d_ff // nsplit
    acc = DN_ALPHA * h_ref[...]
    for j in range(nsplit):
        a = jnp.dot(hb, w1_ref[:, j * step:(j + 1) * step], preferred_element_type=F32)
        a = jnp.square(jnp.maximum(a, 0.0)).astype(BF16)
        acc = acc + jnp.dot(a, w2_ref[j * step:(j + 1) * step, :], preferred_element_type=F32)
    y = _ln_rows(acc, lg_ref[...], lb_ref[...])
    of_ref[...] = y
    ob_ref[...] = y.astype(BF16)


def ffn_call(hb, h, w1, w2, lg, lb, tm=256, nsplit=4):
    t, d = h.shape
    row = lambda: pl.BlockSpec((tm, d), lambda i: (i, 0))
    const = lambda a: pl.BlockSpec(a.shape, lambda i: (0,) * a.ndim)
    vec = pl.BlockSpec((1, d), lambda i: (0, 0))
    return pl.pallas_call(
        functools.partial(_ffn_kernel, nsplit=nsplit),
        grid=(t // tm,),
        in_specs=[row(), row(), const(w1), const(w2), vec, vec],
        out_specs=[row(), row()],
        out_shape=[jax.ShapeDtypeStruct((t, d), F32), jax.ShapeDtypeStruct((t, d), BF16)],
        compiler_params=_cparams("parallel"),
        name="ffn",
    )(hb, h, w1, w2, lg.reshape(1, d), lb.reshape(1, d))


def _split_w_in(w):
    o = 3 * ATTN_W
    conv_ch = D_INNER + 2 * SSD_GROUPS * D_STATE
    w_qkv = w[:, :o]
    w_z = w[:, o:o + D_INNER]
    w_xbc = w[:, o + D_INNER:o + D_INNER + conv_ch]
    w_dt = w[:, o + D_INNER + conv_ch:o + D_INNER + conv_ch + 2 * SSD_HEADS]
    w_gate = w[:, o + D_INNER + conv_ch + 2 * SSD_HEADS:]
    w_dt = w_dt.reshape(-1, 2, SSD_GROUPS, HEADS_PER_GROUP).transpose(0, 2, 1, 3).reshape(-1, 2 * SSD_HEADS)
    w_dt = jnp.pad(w_dt, ((0, 0), (0, LANES - 2 * SSD_HEADS)))
    return (w_qkv.astype(BF16), jnp.concatenate([w_xbc, w_z, w_gate], axis=1).astype(BF16), w_dt.astype(BF16))


def _group_rows(p):
    p = p.astype(F32).reshape(2, SSD_GROUPS, HEADS_PER_GROUP).transpose(1, 0, 2).reshape(SSD_GROUPS, 2 * HEADS_PER_GROUP)
    return jnp.broadcast_to(p[:, :, None], (SSD_GROUPS, 2 * HEADS_PER_GROUP, LANES))


def _mixer_layer(h, hb, bsz, s, w_in, conv_w, conv_b, a_log, dt_bias, d_skip, ssd_norm_w, rpb,
                 w_attn_br, w_ssd_br, b_gate, w_o, ln_g, ln_b):
    t = bsz * s
    w_qkv, w_f, w_dt = _split_w_in(w_in)
    qkv = matmul_call(hb, w_qkv, BF16, 1024, 1024, "proj_qkv")
    pf = matmul_call(hb, w_f, F32, 1024, 1024, "proj_f32")
    dt_raw = matmul_call(hb, w_dt, F32, 1024, LANES, "proj_dt")

    att = natten_call(qkv.reshape(bsz, s, 3 * ATTN_W), build_bias_blocks(rpb))

    dtr = dt_raw[:, :2 * SSD_HEADS].reshape(bsz, s, SSD_GROUPS, 2 * HEADS_PER_GROUP).transpose(0, 2, 3, 1)
    pa = _group_rows(-jnp.exp(a_log.astype(F32)))
    pb = _group_rows(dt_bias)
    dsk = jnp.repeat(d_skip.astype(F32), SSD_HEAD_DIM).reshape(1, D_INNER)
    yn = ssd_call(pf.reshape(bsz, s, -1), dtr, conv_w.astype(F32), conv_b.astype(F32).reshape(1, -1), pa, pb, dsk,
                  ssd_norm_w.astype(F32).reshape(1, D_INNER))

    return merge_call(att.reshape(t, ATTN_W), yn.reshape(t, D_INNER), pf, b_gate.astype(F32), h,
                      w_attn_br.astype(BF16), w_ssd_br.astype(BF16), w_o.astype(BF16), ln_g, ln_b)


def kernel(x, ln0_g, ln0_b, w_in, conv_w, conv_b, a_log, dt_bias, d_skip, ssd_norm_w, rpb, w_attn_br, w_ssd_br,
           b_gate, w_o, ln1_g, ln1_b, w_ff1, w_ff2, ln2_g, ln2_b):
    bsz, s, d = x.shape
    h, hb = layer_norm_call(x.reshape(bsz * s, d), ln0_g, ln0_b)
    for l in range(w_in.shape[0]):
        h, hb = _mixer_layer(h, hb, bsz, s, w_in[l], conv_w[l], conv_b[l], a_log[l], dt_bias[l], d_skip[l],
                             ssd_norm_w[l], rpb[l], w_attn_br[l], w_ssd_br[l], b_gate[l], w_o[l], ln1_g[l], ln1_b[l])
        h, hb = ffn_call(hb, h, w_ff1[l].astype(BF16), w_ff2[l].astype(BF16), ln2_g[l], ln2_b[l])
    return h.reshape(bsz, s, d)
```

```python
import functools
import math

import jax
import jax.numpy as jnp
import numpy as np
from jax import lax
from jax.experimental import pallas as pl
from jax.experimental.pallas import tpu as pltpu

D_MODEL = 1024
DEPTH = 2
GRID_W = 64
NA_HEADS = 16
NA_HEAD_DIM = 64
ATTN_W = NA_HEADS * NA_HEAD_DIM
WIN_R = 8
WIN_C = 16
D_INNER = 2048
SSD_HEAD_DIM = 64
SSD_HEADS = D_INNER // SSD_HEAD_DIM
SSD_GROUPS = 8
HEADS_PER_GROUP = SSD_HEADS // SSD_GROUPS
D_STATE = 128
CONV_K = 5
CHUNK = 128
GROUP_W = D_INNER // SSD_GROUPS
D_FF = 4 * D_MODEL
DN_ALPHA = (2 * DEPTH) ** 0.25
LN_EPS = 1e-5
LOG2E = math.log2(math.e)
RMS_EPS = 1e-5

V7X_VMEM_BYTES = 64 * 1024 * 1024
VMEM_LIMIT = 56 * 1024 * 1024
LANES = 128
NEG_BIG = -1e30

F32 = jnp.float32
BF16 = jnp.bfloat16


def _cparams(*sem):
    return pltpu.CompilerParams(dimension_semantics=sem, vmem_limit_bytes=VMEM_LIMIT)


def _ln_rows(x, g, b):
    mu = jnp.mean(x, axis=-1, keepdims=True)
    xc = x - mu
    var = jnp.mean(xc * xc, axis=-1, keepdims=True)
    return xc * lax.rsqrt(var + LN_EPS) * g + b


def _ln_kernel(x_ref, g_ref, b_ref, of_ref, ob_ref):
    y = _ln_rows(x_ref[...], g_ref[...], b_ref[...])
    of_ref[...] = y
    ob_ref[...] = y.astype(BF16)


def layer_norm_call(x, g, b, tm=512):
    t, d = x.shape
    return pl.pallas_call(
        _ln_kernel,
        grid=(t // tm,),
        in_specs=[pl.BlockSpec((tm, d), lambda i: (i, 0)),
                  pl.BlockSpec((1, d), lambda i: (0, 0)),
                  pl.BlockSpec((1, d), lambda i: (0, 0))],
        out_specs=[pl.BlockSpec((tm, d), lambda i: (i, 0)),
                   pl.BlockSpec((tm, d), lambda i: (i, 0))],
        out_shape=[jax.ShapeDtypeStruct((t, d), F32), jax.ShapeDtypeStruct((t, d), BF16)],
        compiler_params=_cparams("parallel"),
        name="ln0",
    )(x, g.reshape(1, d), b.reshape(1, d))


def _mm_kernel(x_ref, w_ref, o_ref):
    o_ref[...] = jnp.dot(x_ref[...], w_ref[...], preferred_element_type=F32).astype(o_ref.dtype)


def matmul_call(x, w, out_dtype, tm, tn, name):
    t, k = x.shape
    n = w.shape[1]
    return pl.pallas_call(
        _mm_kernel,
        grid=(n // tn, t // tm),
        in_specs=[pl.BlockSpec((tm, k), lambda j, i: (i, 0)),
                  pl.BlockSpec((k, tn), lambda j, i: (0, j))],
        out_specs=pl.BlockSpec((tm, tn), lambda j, i: (i, j)),
        out_shape=jax.ShapeDtypeStruct((t, n), out_dtype),
        compiler_params=_cparams("parallel", "parallel"),
        name=name,
    )(x, w)


def _proj_kernel(x_ref, w_ref, o_ref):
    tn = o_ref.shape[1]
    col = pl.multiple_of(pl.program_id(1) * tn, tn)
    o_ref[...] = jnp.dot(x_ref[...], w_ref[:, pl.ds(col, tn)], preferred_element_type=F32).astype(o_ref.dtype)


def proj_call(x, w, out_dtype, tm, tn, name):
    t, k = x.shape
    n = w.shape[1]
    return pl.pallas_call(
        _proj_kernel,
        grid=(t // tm, n // tn),
        in_specs=[pl.BlockSpec((tm, k), lambda i, j: (i, 0)),
                  pl.BlockSpec((k, n), lambda i, j: (0, 0), pipeline_mode=pl.Buffered(1))],
        out_specs=pl.BlockSpec((tm, tn), lambda i, j: (i, j)),
        out_shape=jax.ShapeDtypeStruct((t, n), out_dtype),
        compiler_params=_cparams("parallel", "arbitrary"),
        name=name,
    )(x, w)


Q_GROUP = 4
K_SLAB = Q_GROUP + WIN_R


def _slab_start(j, rows):
    return jnp.clip(Q_GROUP * j - WIN_R // 2, 0, rows - K_SLAB)


def _group_patterns(rows):
    ngroups = rows // Q_GROUP
    outside = 2 * WIN_R - 1
    pats = []
    for j in range(ngroups):
        ws = min(max(Q_GROUP * j - WIN_R // 2, 0), rows - K_SLAB)
        pat = np.full((Q_GROUP, K_SLAB), outside, np.int64)
        for a in range(Q_GROUP):
            r = Q_GROUP * j + a
            r0 = min(max(r - WIN_R // 2, 0), rows - WIN_R)
            for i in range(K_SLAB):
                if r0 <= ws + i < r0 + WIN_R:
                    pat[a, i] = ws + i - r + WIN_R - 1
        pats.append(pat)
    assert all(np.array_equal(pats[1], p) for p in pats[1:-1])
    return [pats[0].tolist(), pats[1].tolist(), pats[-1].tolist()]


def _natten_kernel(q_ref, k_ref, v_ref, blk_ref, o_ref, bias_s, *, rows):
    ngroups = rows // Q_GROUP
    nq = Q_GROUP * GRID_W
    nk = K_SLAB * GRID_W
    lane = lax.broadcasted_iota(jnp.int32, (nq, LANES), 1)
    first = lane < NA_HEAD_DIM

    @pl.when(pl.program_id(1) == 0)
    def _():
        for v, pat in enumerate(_group_patterns(rows)):
            for hh in range(2):
                for a in range(Q_GROUP):
                    for i in range(0, K_SLAB, 2):
                        pair = jnp.concatenate([blk_ref[hh, pat[a][i]], blk_ref[hh, pat[a][i + 1]]], axis=1)
                        bias_s[v, hh, a * GRID_W:(a + 1) * GRID_W, i * GRID_W:(i + 2) * GRID_W] = pair

    def group_body(j, carry):
        variant = jnp.minimum(j, 1) + jnp.maximum(j - (ngroups - 2), 0)
        qs = pl.multiple_of(j * nq, nq)
        ks = pl.multiple_of(_slab_start(j, rows) * GRID_W, GRID_W)
        q = (q_ref[pl.ds(qs, nq), :].astype(F32) * (NA_HEAD_DIM ** -0.5 * LOG2E)).astype(BF16)
        kw = k_ref[pl.ds(ks, nk), :]
        vw = v_ref[pl.ds(ks, nk), :]
        outs = []
        for hh in range(2):
            keep = first if hh == 0 else jnp.logical_not(first)
            qh = jnp.where(keep, q, jnp.zeros_like(q))
            s = lax.dot_general(qh, kw, (((1,), (1,)), ((), ())), preferred_element_type=F32)
            s = s + bias_s[variant, hh]
            m = jnp.max(s, axis=-1, keepdims=True)
            p = jnp.exp2(s - m)
            l = jnp.sum(p, axis=-1, keepdims=True)
            o = jnp.dot(p.astype(BF16), vw, preferred_element_type=F32)
            outs.append(o / l)
        o_ref[pl.ds(qs, nq), :] = jnp.where(first, outs[0], outs[1]).astype(o_ref.dtype)
        return carry

    lax.fori_loop(0, ngroups, group_body, 0, unroll=2)


def natten_call(qkv, blocks):
    bsz, s, _ = qkv.shape
    rows = s // GRID_W
    assert rows % Q_GROUP == 0 and rows >= K_SLAB + Q_GROUP and K_SLAB % 2 == 0
    npairs = NA_HEADS // 2
    blk = lambda off: pl.BlockSpec((None, s, LANES), lambda hp, b: (b, 0, off + hp))
    return pl.pallas_call(
        functools.partial(_natten_kernel, rows=rows),
        grid=(npairs, bsz),
        in_specs=[blk(0), blk(npairs), blk(2 * npairs),
                  pl.BlockSpec((2,) + blocks.shape[1:], lambda hp, b: (hp, 0, 0, 0))],
        out_specs=pl.BlockSpec((None, s, LANES), lambda hp, b: (b, 0, hp)),
        out_shape=jax.ShapeDtypeStruct((bsz, s, ATTN_W), BF16),
        scratch_shapes=[pltpu.VMEM((3, 2, Q_GROUP * GRID_W, K_SLAB * GRID_W), F32)],
        compiler_params=_cparams("arbitrary", "arbitrary"),
        name="natten",
    )(qkv, qkv, qkv, blocks)


def build_bias_blocks(rpb):
    w = np.arange(GRID_W)[:, None]
    kc = np.arange(GRID_W)[None, :]
    c0 = np.clip(w - WIN_C // 2, 0, GRID_W - WIN_C)
    col_ok = (kc >= c0) & (kc < c0 + WIN_C)
    col_sel = np.zeros((GRID_W, GRID_W, 2 * WIN_C - 1), np.float32)
    col_sel[w, kc, np.clip(kc - w + WIN_C - 1, 0, 2 * WIN_C - 2)] = col_ok
    blocks = jnp.einsum("hrd,wkd->hrwk", rpb.astype(F32) * LOG2E, col_sel, precision=lax.Precision.HIGHEST)
    blocks = jnp.where(col_ok[None, None], blocks, NEG_BIG)
    return jnp.concatenate([blocks, jnp.full_like(blocks[:, :1], NEG_BIG)], axis=1)


def _silu(x):
    return x * jax.nn.sigmoid(x)


def _softplus(x):
    return jnp.maximum(x, 0.0) + jnp.log1p(jnp.exp(-jnp.abs(x)))


def _chunk_scan(x, reverse):
    width = x.shape[1]
    lane = lax.broadcasted_iota(jnp.int32, x.shape, 1) & (CHUNK - 1)
    sh = 1
    while sh < CHUNK:
        if reverse:
            x = x + jnp.where(lane < CHUNK - sh, pltpu.roll(x, width - sh, 1), 0.0)
        else:
            x = x + jnp.where(lane >= sh, pltpu.roll(x, sh, 1), 0.0)
        sh *= 2
    return x


def _expand_heads_row(rows):
    lane = lax.broadcasted_iota(jnp.int32, (1, LANES), 1)
    first = lane < SSD_HEAD_DIM
    return jnp.concatenate([jnp.where(first, rows[0], rows[1]), jnp.where(first, rows[2], rows[3])], axis=1)


def _ssd_kernel(xs_ref, bm_ref, cm_ref, z_ref, dtr_ref, cwx_ref, cwb_ref, cwc_ref, cbx_ref, cbb_ref, cbc_ref,
                pa_ref, pb_ref, dsk_ref, nw_ref, spread_ref, o_ref,
                pad_x, pad_b, pad_c, xbd_s, bc_s, cc_s, bt_s, cb_s, rows_s, csb_s, y_s, st_s,
                *, seq):
    nchunk = seq // CHUNK
    halo = 8
    nh = HEADS_PER_GROUP
    lane256 = lax.broadcasted_iota(jnp.int32, (CHUNK, GROUP_W), 1)
    head_masks = [(lane256 >= SSD_HEAD_DIM * r) & (lane256 < SSD_HEAD_DIM * (r + 1)) for r in range(nh)]

    a_rows = jnp.concatenate([pa_ref[...]] * nchunk, axis=1)
    bias_rows = jnp.concatenate([pb_ref[...]] * nchunk, axis=1)
    is_bwd = lax.broadcasted_iota(jnp.int32, (2 * nh, seq), 0) >= nh
    dt = _softplus(dtr_ref[...] + bias_rows)
    adt = dt * a_rows
    pre = _chunk_scan(adt, reverse=False)
    suf = _chunk_scan(adt, reverse=True)
    cs = jnp.where(is_bwd, suf, pre)
    total = pre + suf - adt
    rows_s[0] = (cs - jnp.log(dt)) * LOG2E
    rows_s[1] = jnp.exp(total)
    rows_s[2] = dt * jnp.exp(total - cs)
    cs2 = cs * LOG2E
    hi = cs2.astype(BF16).astype(F32)
    mid = (cs2 - hi).astype(BF16).astype(F32)
    rows_s[3] = hi
    rows_s[4] = mid
    rows_s[5] = ((cs2 - hi) - mid).astype(BF16).astype(F32)

    def pad_copy(src_ref, pad_ref):
        width = src_ref.shape[-1]
        pad_ref[pl.ds(0, halo), :] = jnp.zeros((halo, width), F32)
        pad_ref[pl.ds(halo + seq, halo), :] = jnp.zeros((halo, width), F32)
        pad_ref[pl.ds(halo, seq), :] = src_ref[...].astype(F32)

    def conv_chunk(pad_ref, w_ref, b_ref, t0):
        acc = jnp.broadcast_to(b_ref[...], (CHUNK, pad_ref.shape[-1]))
        for k in range(CONV_K):
            acc = acc + pad_ref[pl.ds(t0 + halo + k - CONV_K // 2, CHUNK), :] * w_ref[k:k + 1, :]
        return _silu(acc)

    pad_copy(bm_ref, pad_b)
    pad_copy(cm_ref, pad_c)
    pad_copy(xs_ref, pad_x)
    for c in range(nchunk):
        t0 = c * CHUNK
        vb = conv_chunk(pad_b, cwb_ref, cbb_ref, t0)
        vc = conv_chunk(pad_c, cwc_ref, cbc_ref, t0).astype(BF16)
        vx = conv_chunk(pad_x, cwx_ref, cbx_ref, t0)
        bc_s[pl.ds(t0, CHUNK), :] = vb.astype(BF16)
        bt_s[:, pl.ds(t0, CHUNK)] = vb.T.astype(BF16)
        cc_s[pl.ds(t0, CHUNK), :] = vc
        y_s[pl.ds(t0, CHUNK), :] = vx * dsk_ref[...]
        vxb = vx.astype(BF16)
        for r in range(nh):
            xbd_s[pl.ds((c * nh + r) * CHUNK, CHUNK), :] = jnp.where(head_masks[r], vxb, jnp.zeros_like(vxb))
        cb_s[pl.ds(t0, CHUNK), :] = lax.dot_general(vc, vb.astype(BF16), (((1,), (1,)), ((), ())),
                                                    preferred_element_type=F32)
        pieces = jnp.concatenate([rows_s[3, :, pl.ds(t0, CHUNK)], rows_s[4, :, pl.ds(t0, CHUNK)],
                                  rows_s[5, :, pl.ds(t0, CHUNK)], jnp.zeros((CHUNK - 6 * nh, CHUNK), F32)], axis=0)
        csb_s[pl.ds(t0, CHUNK), :] = jnp.dot(pieces.T.astype(BF16), spread_ref[...], preferred_element_type=F32)

    row_i = lax.broadcasted_iota(jnp.int32, (CHUNK, CHUNK), 0)
    col_i = lax.broadcasted_iota(jnp.int32, (CHUNK, CHUNK), 1)
    st_s[...] = jnp.zeros(st_s.shape, F32)

    def scan_step(direction, c):
        causal = (row_i >= col_i) if direction == 0 else (row_i <= col_i)
        t0 = pl.multiple_of(c * CHUNK, CHUNK)
        ck = cc_s[pl.ds(t0, CHUNK), :]
        cb = cb_s[pl.ds(t0, CHUNK), :]
        bt = bt_s[:, pl.ds(t0, CHUNK)]
        src = rows_s[0, :, pl.ds(t0, CHUNK)]
        decay = rows_s[1, :, pl.ds(t0, CHUNK)]
        w_state = rows_s[2, :, pl.ds(t0, CHUNK)].astype(BF16)
        j0 = direction * nh
        ms, bts, w_out = [], [], []
        for r in range(nh):
            cs_col = csb_s[pl.ds(t0, CHUNK), (j0 + r) * LANES:(j0 + r + 1) * LANES]
            seg = cs_col - src[j0 + r:j0 + r + 1, :]
            ms.append((cb * jnp.exp2(jnp.where(causal, seg, NEG_BIG))).astype(BF16))
            bts.append(bt * w_state[j0 + r:j0 + r + 1, :])
            w_out.append(jnp.exp2(cs_col))
        lhs = jnp.concatenate([jnp.concatenate(ms, axis=1), jnp.concatenate(bts, axis=1)], axis=0)
        xbd = xbd_s[pl.ds(pl.multiple_of(c * (nh * CHUNK), nh * CHUNK), nh * CHUNK), :]
        both = jnp.dot(lhs, xbd, preferred_element_type=F32)
        first = col_i < SSD_HEAD_DIM
        w_out_x = jnp.concatenate([jnp.where(first, w_out[0], w_out[1]), jnp.where(first, w_out[2], w_out[3])], axis=1)
        prev = st_s[direction]
        y = both[:CHUNK] + jnp.dot(ck, prev.astype(BF16), preferred_element_type=F32) * w_out_x
        decay_x = _expand_heads_row([decay[j0 + r:j0 + r + 1, :] for r in range(nh)])
        st_s[direction] = prev * decay_x + both[CHUNK:]
        y_s[pl.ds(t0, CHUNK), :] = y_s[pl.ds(t0, CHUNK), :] + y

    def scan_body(i, carry):
        scan_step(0, i)
        scan_step(1, nchunk - 1 - i)
        return carry

    lax.fori_loop(0, nchunk, scan_body, 0, unroll=4)

    def fin(c, carry):
        t0 = pl.multiple_of(c * CHUNK, CHUNK)
        y = y_s[pl.ds(t0, CHUNK), :] * _silu(z_ref[pl.ds(t0, CHUNK), :].astype(F32))
        y = y * lax.rsqrt(jnp.mean(y * y, axis=-1, keepdims=True) + RMS_EPS)
        o_ref[pl.ds(t0, CHUNK), :] = (y * nw_ref[...]).astype(o_ref.dtype)
        return carry

    lax.fori_loop(0, nchunk, fin, 0, unroll=4)


def ssd_call(pf, col0, dtr, cw, cb, pa, pb, dsk, nw):
    bsz, s, _ = pf.shape
    g_ = SSD_GROUPS
    nh = HEADS_PER_GROUP
    xo, bo = col0 // GROUP_W, (col0 + D_INNER) // D_STATE
    co, zo = (col0 + D_INNER + g_ * D_STATE) // D_STATE, (col0 + D_INNER + 2 * g_ * D_STATE) // GROUP_W
    cwx, cwb, cwc = cw[:, :D_INNER], cw[:, D_INNER:D_INNER + g_ * D_STATE], cw[:, D_INNER + g_ * D_STATE:]
    cbx, cbb, cbc = cb[:, :D_INNER], cb[:, D_INNER:D_INNER + g_ * D_STATE], cb[:, D_INNER + g_ * D_STATE:]
    wide = lambda off: pl.BlockSpec((None, s, GROUP_W), lambda b, g: (b, 0, off + g))
    narrow = lambda off: pl.BlockSpec((None, s, D_STATE), lambda b, g: (b, 0, off + g))
    pw = lambda rows, width: pl.BlockSpec((rows, width), lambda b, g: (0, g))
    spread = np.zeros((LANES, 2 * nh * LANES), np.float32)
    for p in range(3):
        for j in range(2 * nh):
            spread[2 * nh * p + j, j * LANES:(j + 1) * LANES] = 1.0
    spread = jnp.asarray(spread, BF16)
    return pl.pallas_call(
        functools.partial(_ssd_kernel, seq=s),
        grid=(bsz, g_),
        in_specs=[wide(xo), narrow(bo), narrow(co), wide(zo),
                  pl.BlockSpec((None, None, 2 * nh, s), lambda b, g: (b, g, 0, 0)),
                  pw(CONV_K, GROUP_W), pw(CONV_K, D_STATE), pw(CONV_K, D_STATE),
                  pw(1, GROUP_W), pw(1, D_STATE), pw(1, D_STATE),
                  pl.BlockSpec((None, 2 * nh, LANES), lambda b, g: (g, 0, 0)),
                  pl.BlockSpec((None, 2 * nh, LANES), lambda b, g: (g, 0, 0)),
                  pw(1, GROUP_W), pw(1, GROUP_W),
                  pl.BlockSpec(spread.shape, lambda b, g: (0, 0))],
        out_specs=pl.BlockSpec((None, s, GROUP_W), lambda b, g: (b, 0, g)),
        out_shape=jax.ShapeDtypeStruct((bsz, s, D_INNER), BF16),
        scratch_shapes=[pltpu.VMEM((s + 16, GROUP_W), F32),
                        pltpu.VMEM((s + 16, D_STATE), F32),
                        pltpu.VMEM((s + 16, D_STATE), F32),
                        pltpu.VMEM((nh * s, GROUP_W), BF16),
                        pltpu.VMEM((s, D_STATE), BF16),
                        pltpu.VMEM((s, D_STATE), BF16),
                        pltpu.VMEM((D_STATE, s), BF16),
                        pltpu.VMEM((s, CHUNK), F32),
                        pltpu.VMEM((6, 2 * nh, s), F32),
                        pltpu.VMEM((s, 2 * nh * LANES), F32),
                        pltpu.VMEM((s, GROUP_W), F32),
                        pltpu.VMEM((2, D_STATE, GROUP_W), F32)],
        compiler_params=_cparams("parallel", "parallel"),
        name="ssd",
    )(pf, pf, pf, pf, dtr, cwx, cwb, cwc, cbx, cbb, cbc, pa, pb, dsk, nw, spread)


def _merge_kernel(att_ref, yn_ref, g0_ref, g1_ref, bg0_ref, bg1_ref, h_ref, wa_ref, ws_ref, wo_ref,
                  lg_ref, lb_ref, of_ref, ob_ref):
    ya = jnp.dot(att_ref[...], wa_ref[...], preferred_element_type=F32)
    ys = jnp.dot(yn_ref[...], ws_ref[...], preferred_element_type=F32)
    merged = (jax.nn.sigmoid(g0_ref[...].astype(F32) + bg0_ref[...]) * ya
              + jax.nn.sigmoid(g1_ref[...].astype(F32) + bg1_ref[...]) * ys)
    mix = jnp.dot(merged.astype(BF16), wo_ref[...], preferred_element_type=F32)
    y = _ln_rows(DN_ALPHA * h_ref[...] + mix, lg_ref[...], lb_ref[...])
    of_ref[...] = y
    ob_ref[...] = y.astype(BF16)


def merge_call(att, yn, pf, gate_col, b_gate, h, wa, ws, wo, lg, lb, tm=512):
    t, d = h.shape
    gate_off = gate_col // d
    row = lambda width: pl.BlockSpec((tm, width), lambda i: (i, 0))
    const = lambda a: pl.BlockSpec(a.shape, lambda i: (0,) * a.ndim, pipeline_mode=pl.Buffered(1))
    bg = b_gate.reshape(1, 2 * d)
    return pl.pallas_call(
        _merge_kernel,
        grid=(t // tm,),
        in_specs=[row(d), row(yn.shape[1]),
                  pl.BlockSpec((tm, d), lambda i: (i, gate_off)),
                  pl.BlockSpec((tm, d), lambda i: (i, gate_off + 1)),
                  pl.BlockSpec((1, d), lambda i: (0, 0)), pl.BlockSpec((1, d), lambda i: (0, 1)),
                  row(d), const(wa), const(ws), const(wo),
                  pl.BlockSpec((1, d), lambda i: (0, 0)), pl.BlockSpec((1, d), lambda i: (0, 0))],
        out_specs=[row(d), row(d)],
        out_shape=[jax.ShapeDtypeStruct((t, d), F32), jax.ShapeDtypeStruct((t, d), BF16)],
        compiler_params=_cparams("parallel"),
        name="merge",
    )(att, yn, pf, pf, bg, bg, h, wa, ws, wo, lg.reshape(1, d), lb.reshape(1, d))


def _ffn_kernel(hb_ref, h_ref, w1_ref, w2_ref, lg_ref, lb_ref, of_ref, ob_ref, *, nsplit):
    hb = hb_ref[...]
    d_ff = w1_ref.shape[1]
    step = d_ff // nsplit
    acc = DN_ALPHA * h_ref[...]
    for j in range(nsplit):
        a = jnp.dot(hb, w1_ref[:, j * step:(j + 1) * step], preferred_element_type=F32)
        a = jnp.square(jnp.maximum(a, 0.0)).astype(BF16)
        acc = acc + jnp.dot(a, w2_ref[j * step:(j + 1) * step, :], preferred_element_type=F32)
    y = _ln_rows(acc, lg_ref[...], lb_ref[...])
    of_ref[...] = y
    ob_ref[...] = y.astype(BF16)


def ffn_call(hb, h, w1, w2, lg, lb, tm=512, nsplit=4):
    t, d = h.shape
    row = lambda: pl.BlockSpec((tm, d), lambda i: (i, 0))
    const = lambda a: pl.BlockSpec(a.shape, lambda i: (0,) * a.ndim, pipeline_mode=pl.Buffered(1))
    vec = pl.BlockSpec((1, d), lambda i: (0, 0))
    return pl.pallas_call(
        functools.partial(_ffn_kernel, nsplit=nsplit),
        grid=(t // tm,),
        in_specs=[row(), row(), const(w1), const(w2), vec, vec],
        out_specs=[row(), row()],
        out_shape=[jax.ShapeDtypeStruct((t, d), F32), jax.ShapeDtypeStruct((t, d), BF16)],
        compiler_params=_cparams("parallel"),
        name="ffn",
    )(hb, h, w1, w2, lg.reshape(1, d), lb.reshape(1, d))


def _split_w_in(w):
    o = 3 * ATTN_W
    conv_ch = D_INNER + 2 * SSD_GROUPS * D_STATE
    w_qkv = w[:, :o]
    w_z = w[:, o:o + D_INNER]
    w_xbc = w[:, o + D_INNER:o + D_INNER + conv_ch]
    w_dt = w[:, o + D_INNER + conv_ch:o + D_INNER + conv_ch + 2 * SSD_HEADS]
    w_gate = w[:, o + D_INNER + conv_ch + 2 * SSD_HEADS:]
    w_dt = w_dt.reshape(-1, 2, SSD_GROUPS, HEADS_PER_GROUP).transpose(0, 2, 1, 3).reshape(-1, 2 * SSD_HEADS)
    w_dt = jnp.pad(w_dt, ((0, 0), (0, LANES - 2 * SSD_HEADS)))
    return jnp.concatenate([w_qkv, w_xbc, w_z, w_gate], axis=1).astype(BF16), w_dt.astype(BF16)


def _group_rows(p):
    p = p.astype(F32).reshape(2, SSD_GROUPS, HEADS_PER_GROUP).transpose(1, 0, 2).reshape(SSD_GROUPS, 2 * HEADS_PER_GROUP)
    return jnp.broadcast_to(p[:, :, None], (SSD_GROUPS, 2 * HEADS_PER_GROUP, LANES))


def _mixer_layer(h, hb, bsz, s, w_in, conv_w, conv_b, a_log, dt_bias, d_skip, ssd_norm_w, rpb,
                 w_attn_br, w_ssd_br, b_gate, w_o, ln_g, ln_b):
    t = bsz * s
    w_all, w_dt = _split_w_in(w_in)
    ssd_col = 3 * ATTN_W
    gate_col = ssd_col + 2 * D_INNER + 2 * SSD_GROUPS * D_STATE
    pf = proj_call(hb, w_all, BF16, 1024, 1024, "proj")
    dt_raw = matmul_call(hb, w_dt, F32, 1024, LANES, "proj_dt")

    att = natten_call(pf.reshape(bsz, s, -1), build_bias_blocks(rpb))

    dtr = dt_raw[:, :2 * SSD_HEADS].reshape(bsz, s, SSD_GROUPS, 2 * HEADS_PER_GROUP).transpose(0, 2, 3, 1)
    pa = _group_rows(-jnp.exp(a_log.astype(F32)))
    pb = _group_rows(dt_bias)
    dsk = jnp.repeat(d_skip.astype(F32), SSD_HEAD_DIM).reshape(1, D_INNER)
    yn = ssd_call(pf.reshape(bsz, s, -1), ssd_col, dtr, conv_w.astype(F32), conv_b.astype(F32).reshape(1, -1),
                  pa, pb, dsk, ssd_norm_w.astype(F32).reshape(1, D_INNER))

    return merge_call(att.reshape(t, ATTN_W), yn.reshape(t, D_INNER), pf, gate_col, b_gate.astype(F32), h,
                      w_attn_br.astype(BF16), w_ssd_br.astype(BF16), w_o.astype(BF16), ln_g, ln_b)


def kernel(x, ln0_g, ln0_b, w_in, conv_w, conv_b, a_log, dt_bias, d_skip, ssd_norm_w, rpb, w_attn_br, w_ssd_br,
           b_gate, w_o, ln1_g, ln1_b, w_ff1, w_ff2, ln2_g, ln2_b):
    bsz, s, d = x.shape
    h, hb = layer_norm_call(x.reshape(bsz * s, d), ln0_g, ln0_b)
    for l in range(w_in.shape[0]):
        h, hb = _mixer_layer(h, hb, bsz, s, w_in[l], conv_w[l], conv_b[l], a_log[l], dt_bias[l], d_skip[l],
                             ssd_norm_w[l], rpb[l], w_attn_br[l], w_ssd_br[l], b_gate[l], w_o[l], ln1_g[l], ln1_b[l])
        h, hb = ffn_call(hb, h, w_ff1[l].astype(BF16), w_ff2[l].astype(BF16), ln2_g[l], ln2_b[l])
    return h.reshape(bsz, s, d)
```

```python
import functools
import math

import jax
import jax.numpy as jnp
import numpy as np
from jax import lax
from jax.experimental import pallas as pl
from jax.experimental.pallas import tpu as pltpu

D_MODEL = 1024
DEPTH = 2
GRID_W = 64
NA_HEADS = 16
NA_HEAD_DIM = 64
ATTN_W = NA_HEADS * NA_HEAD_DIM
WIN_R = 8
WIN_C = 16
D_INNER = 2048
SSD_HEAD_DIM = 64
SSD_HEADS = D_INNER // SSD_HEAD_DIM
SSD_GROUPS = 8
HEADS_PER_GROUP = SSD_HEADS // SSD_GROUPS
D_STATE = 128
CONV_K = 5
CHUNK = 128
GROUP_W = D_INNER // SSD_GROUPS
D_FF = 4 * D_MODEL
DN_ALPHA = (2 * DEPTH) ** 0.25
LN_EPS = 1e-5
LOG2E = math.log2(math.e)
RMS_EPS = 1e-5

V7X_VMEM_BYTES = 64 * 1024 * 1024
VMEM_LIMIT = 56 * 1024 * 1024
LANES = 128
NEG_BIG = -1e30

F32 = jnp.float32
BF16 = jnp.bfloat16


def _cparams(*sem):
    return pltpu.CompilerParams(dimension_semantics=sem, vmem_limit_bytes=VMEM_LIMIT)


def _ln_rows(x, g, b):
    mu = jnp.mean(x, axis=-1, keepdims=True)
    xc = x - mu
    var = jnp.mean(xc * xc, axis=-1, keepdims=True)
    return xc * lax.rsqrt(var + LN_EPS) * g + b


def _ln_kernel(x_ref, g_ref, b_ref, of_ref, ob_ref):
    y = _ln_rows(x_ref[...], g_ref[...], b_ref[...])
    of_ref[...] = y
    ob_ref[...] = y.astype(BF16)


def layer_norm_call(x, g, b, tm=512):
    t, d = x.shape
    return pl.pallas_call(
        _ln_kernel,
        grid=(t // tm,),
        in_specs=[pl.BlockSpec((tm, d), lambda i: (i, 0)),
                  pl.BlockSpec((1, d), lambda i: (0, 0)),
                  pl.BlockSpec((1, d), lambda i: (0, 0))],
        out_specs=[pl.BlockSpec((tm, d), lambda i: (i, 0)),
                   pl.BlockSpec((tm, d), lambda i: (i, 0))],
        out_shape=[jax.ShapeDtypeStruct((t, d), F32), jax.ShapeDtypeStruct((t, d), BF16)],
        compiler_params=_cparams("parallel"),
        name="ln0",
    )(x, g.reshape(1, d), b.reshape(1, d))


def _proj_t_kernel(x_ref, w_ref, o_ref):
    o_ref[...] = lax.dot_general(w_ref[...], x_ref[...], (((1,), (1,)), ((), ())), preferred_element_type=F32)


def proj_t_call(x, w_t, bsz, name):
    t, k = x.shape
    s = t // bsz
    r = w_t.shape[0]
    return pl.pallas_call(
        _proj_t_kernel,
        grid=(bsz,),
        in_specs=[pl.BlockSpec((s, k), lambda b: (b, 0)),
                  pl.BlockSpec((r, k), lambda b: (0, 0))],
        out_specs=pl.BlockSpec((None, r, s), lambda b: (b, 0, 0)),
        out_shape=jax.ShapeDtypeStruct((bsz, r, s), F32),
        compiler_params=_cparams("parallel"),
        name=name,
    )(x, w_t)


def _proj_kernel(x_ref, w_ref, o_ref):
    tn = o_ref.shape[1]
    col = pl.multiple_of(pl.program_id(1) * tn, tn)
    o_ref[...] = jnp.dot(x_ref[...], w_ref[:, pl.ds(col, tn)], preferred_element_type=F32).astype(o_ref.dtype)


def proj_call(x, w, out_dtype, tm, tn, name):
    t, k = x.shape
    n = w.shape[1]
    return pl.pallas_call(
        _proj_kernel,
        grid=(t // tm, n // tn),
        in_specs=[pl.BlockSpec((tm, k), lambda i, j: (i, 0)),
                  pl.BlockSpec((k, n), lambda i, j: (0, 0), pipeline_mode=pl.Buffered(1))],
        out_specs=pl.BlockSpec((tm, tn), lambda i, j: (i, j)),
        out_shape=jax.ShapeDtypeStruct((t, n), out_dtype),
        compiler_params=_cparams("parallel", "arbitrary"),
        name=name,
    )(x, w)


Q_GROUP = 4
K_SLAB = Q_GROUP + WIN_R


def _slab_start(j, rows):
    return jnp.clip(Q_GROUP * j - WIN_R // 2, 0, rows - K_SLAB)


def _group_patterns(rows):
    ngroups = rows // Q_GROUP
    outside = 2 * WIN_R - 1
    pats = []
    for j in range(ngroups):
        ws = min(max(Q_GROUP * j - WIN_R // 2, 0), rows - K_SLAB)
        pat = np.full((Q_GROUP, K_SLAB), outside, np.int64)
        for a in range(Q_GROUP):
            r = Q_GROUP * j + a
            r0 = min(max(r - WIN_R // 2, 0), rows - WIN_R)
            for i in range(K_SLAB):
                if r0 <= ws + i < r0 + WIN_R:
                    pat[a, i] = ws + i - r + WIN_R - 1
        pats.append(pat)
    assert all(np.array_equal(pats[1], p) for p in pats[1:-1])
    return [pats[0].tolist(), pats[1].tolist(), pats[-1].tolist()]


def _natten_kernel(q_ref, k_ref, v_ref, blk_ref, o_ref, bias_s, *, rows):
    ngroups = rows // Q_GROUP
    nq = Q_GROUP * GRID_W
    nk = K_SLAB * GRID_W
    lane = lax.broadcasted_iota(jnp.int32, (nq, LANES), 1)
    first = lane < NA_HEAD_DIM

    @pl.when(pl.program_id(1) == 0)
    def _():
        for v, pat in enumerate(_group_patterns(rows)):
            for hh in range(2):
                for a in range(Q_GROUP):
                    for i in range(0, K_SLAB, 2):
                        pair = jnp.concatenate([blk_ref[hh, pat[a][i]], blk_ref[hh, pat[a][i + 1]]], axis=1)
                        bias_s[v, hh, a * GRID_W:(a + 1) * GRID_W, i * GRID_W:(i + 2) * GRID_W] = pair

    def group_body(j, carry):
        variant = jnp.minimum(j, 1) + jnp.maximum(j - (ngroups - 2), 0)
        qs = pl.multiple_of(j * nq, nq)
        ks = pl.multiple_of(_slab_start(j, rows) * GRID_W, GRID_W)
        q = (q_ref[pl.ds(qs, nq), :].astype(F32) * (NA_HEAD_DIM ** -0.5 * LOG2E)).astype(BF16)
        kw = k_ref[pl.ds(ks, nk), :]
        vw = v_ref[pl.ds(ks, nk), :]
        outs = []
        for hh in range(2):
            keep = first if hh == 0 else jnp.logical_not(first)
            qh = jnp.where(keep, q, jnp.zeros_like(q))
            s = lax.dot_general(qh, kw, (((1,), (1,)), ((), ())), preferred_element_type=F32)
            s = s + bias_s[variant, hh]
            m = jnp.max(s, axis=-1, keepdims=True)
            p = jnp.exp2(s - m)
            l = jnp.sum(p, axis=-1, keepdims=True)
            o = jnp.dot(p.astype(BF16), vw, preferred_element_type=F32)
            outs.append(o / l)
        o_ref[pl.ds(qs, nq), :] = jnp.where(first, outs[0], outs[1]).astype(o_ref.dtype)
        return carry

    lax.fori_loop(0, ngroups, group_body, 0, unroll=2)


def natten_call(qkv, blocks):
    bsz, s, _ = qkv.shape
    rows = s // GRID_W
    assert rows % Q_GROUP == 0 and rows >= K_SLAB + Q_GROUP and K_SLAB % 2 == 0
    npairs = NA_HEADS // 2
    blk = lambda off: pl.BlockSpec((None, s, LANES), lambda hp, b: (b, 0, off + hp))
    return pl.pallas_call(
        functools.partial(_natten_kernel, rows=rows),
        grid=(npairs, bsz),
        in_specs=[blk(0), blk(npairs), blk(2 * npairs),
                  pl.BlockSpec((2,) + blocks.shape[1:], lambda hp, b: (hp, 0, 0, 0))],
        out_specs=pl.BlockSpec((None, s, LANES), lambda hp, b: (b, 0, hp)),
        out_shape=jax.ShapeDtypeStruct((bsz, s, ATTN_W), BF16),
        scratch_shapes=[pltpu.VMEM((3, 2, Q_GROUP * GRID_W, K_SLAB * GRID_W), F32)],
        compiler_params=_cparams("arbitrary", "arbitrary"),
        name="natten",
    )(qkv, qkv, qkv, blocks)


def build_bias_blocks(rpb):
    w = np.arange(GRID_W)[:, None]
    kc = np.arange(GRID_W)[None, :]
    c0 = np.clip(w - WIN_C // 2, 0, GRID_W - WIN_C)
    col_ok = (kc >= c0) & (kc < c0 + WIN_C)
    col_sel = np.zeros((GRID_W, GRID_W, 2 * WIN_C - 1), np.float32)
    col_sel[w, kc, np.clip(kc - w + WIN_C - 1, 0, 2 * WIN_C - 2)] = col_ok
    blocks = jnp.einsum("hrd,wkd->hrwk", rpb.astype(F32) * LOG2E, col_sel, precision=lax.Precision.HIGHEST)
    blocks = jnp.where(col_ok[None, None], blocks, NEG_BIG)
    return jnp.concatenate([blocks, jnp.full_like(blocks[:, :1], NEG_BIG)], axis=1)


def _silu(x):
    return x * jax.nn.sigmoid(x)


def _softplus(x):
    return jnp.maximum(x, 0.0) + jnp.log1p(jnp.exp(-jnp.abs(x)))


def _chunk_scan(x, reverse):
    width = x.shape[1]
    lane = lax.broadcasted_iota(jnp.int32, x.shape, 1) & (CHUNK - 1)
    sh = 1
    while sh < CHUNK:
        if reverse:
            x = x + jnp.where(lane < CHUNK - sh, pltpu.roll(x, width - sh, 1), 0.0)
        else:
            x = x + jnp.where(lane >= sh, pltpu.roll(x, sh, 1), 0.0)
        sh *= 2
    return x


def _expand_heads_row(rows):
    lane = lax.broadcasted_iota(jnp.int32, (1, LANES), 1)
    first = lane < SSD_HEAD_DIM
    return jnp.concatenate([jnp.where(first, rows[0], rows[1]), jnp.where(first, rows[2], rows[3])], axis=1)


def _ssd_kernel(xs_ref, bm_ref, cm_ref, z_ref, dtr_ref, cwx_ref, cwb_ref, cwc_ref, cbx_ref, cbb_ref, cbc_ref,
                pa_ref, pb_ref, dsk_ref, nw_ref, spread_ref, o_ref,
                pad_x, pad_b, pad_c, xbd_s, bc_s, cc_s, bt_s, cb_s, rows_s, csb_s, y_s, st_s,
                *, seq):
    nchunk = seq // CHUNK
    halo = 8
    nh = HEADS_PER_GROUP
    lane256 = lax.broadcasted_iota(jnp.int32, (CHUNK, GROUP_W), 1)
    head_masks = [(lane256 >= SSD_HEAD_DIM * r) & (lane256 < SSD_HEAD_DIM * (r + 1)) for r in range(nh)]

    a_rows = jnp.concatenate([pa_ref[...]] * nchunk, axis=1)
    bias_rows = jnp.concatenate([pb_ref[...]] * nchunk, axis=1)
    is_bwd = lax.broadcasted_iota(jnp.int32, (2 * nh, seq), 0) >= nh
    dt = _softplus(dtr_ref[...] + bias_rows)
    adt = dt * a_rows
    pre = _chunk_scan(adt, reverse=False)
    suf = _chunk_scan(adt, reverse=True)
    cs = jnp.where(is_bwd, suf, pre)
    total = pre + suf - adt
    rows_s[0] = (cs - jnp.log(dt)) * LOG2E
    rows_s[1] = jnp.exp(total)
    rows_s[2] = dt * jnp.exp(total - cs)
    cs2 = cs * LOG2E
    hi = cs2.astype(BF16).astype(F32)
    mid = (cs2 - hi).astype(BF16).astype(F32)
    rows_s[3] = hi
    rows_s[4] = mid
    rows_s[5] = ((cs2 - hi) - mid).astype(BF16).astype(F32)

    def pad_copy(src_ref, pad_ref):
        width = src_ref.shape[-1]
        pad_ref[pl.ds(0, halo), :] = jnp.zeros((halo, width), F32)
        pad_ref[pl.ds(halo + seq, halo), :] = jnp.zeros((halo, width), F32)
        pad_ref[pl.ds(halo, seq), :] = src_ref[...].astype(F32)

    def conv_chunk(pad_ref, w_ref, b_ref, t0):
        acc = jnp.broadcast_to(b_ref[...], (CHUNK, pad_ref.shape[-1]))
        for k in range(CONV_K):
            acc = acc + pad_ref[pl.ds(t0 + halo + k - CONV_K // 2, CHUNK), :] * w_ref[k:k + 1, :]
        return _silu(acc)

    pad_copy(bm_ref, pad_b)
    pad_copy(cm_ref, pad_c)
    pad_copy(xs_ref, pad_x)
    for c in range(nchunk):
        t0 = c * CHUNK
        vb = conv_chunk(pad_b, cwb_ref, cbb_ref, t0)
        vc = conv_chunk(pad_c, cwc_ref, cbc_ref, t0).astype(BF16)
        vx = conv_chunk(pad_x, cwx_ref, cbx_ref, t0)
        bc_s[pl.ds(t0, CHUNK), :] = vb.astype(BF16)
        bt_s[:, pl.ds(t0, CHUNK)] = vb.T.astype(BF16)
        cc_s[pl.ds(t0, CHUNK), :] = vc
        y_s[pl.ds(t0, CHUNK), :] = vx * dsk_ref[...]
        vxb = vx.astype(BF16)
        for r in range(nh):
            xbd_s[pl.ds((c * nh + r) * CHUNK, CHUNK), :] = jnp.where(head_masks[r], vxb, jnp.zeros_like(vxb))
        cb_s[pl.ds(t0, CHUNK), :] = lax.dot_general(vc, vb.astype(BF16), (((1,), (1,)), ((), ())),
                                                    preferred_element_type=F32)
        pieces = jnp.concatenate([rows_s[3, :, pl.ds(t0, CHUNK)], rows_s[4, :, pl.ds(t0, CHUNK)],
                                  rows_s[5, :, pl.ds(t0, CHUNK)], jnp.zeros((CHUNK - 6 * nh, CHUNK), F32)], axis=0)
        csb_s[pl.ds(t0, CHUNK), :] = jnp.dot(pieces.T.astype(BF16), spread_ref[...], preferred_element_type=F32)

    row_i = lax.broadcasted_iota(jnp.int32, (CHUNK, CHUNK), 0)
    col_i = lax.broadcasted_iota(jnp.int32, (CHUNK, CHUNK), 1)
    st_s[...] = jnp.zeros(st_s.shape, F32)

    def scan_step(direction, c):
        causal = (row_i >= col_i) if direction == 0 else (row_i <= col_i)
        t0 = pl.multiple_of(c * CHUNK, CHUNK)
        ck = cc_s[pl.ds(t0, CHUNK), :]
        cb = cb_s[pl.ds(t0, CHUNK), :]
        bt = bt_s[:, pl.ds(t0, CHUNK)]
        src = rows_s[0, :, pl.ds(t0, CHUNK)]
        decay = rows_s[1, :, pl.ds(t0, CHUNK)]
        w_state = rows_s[2, :, pl.ds(t0, CHUNK)].astype(BF16)
        j0 = direction * nh
        ms, bts, w_out = [], [], []
        for r in range(nh):
            cs_col = csb_s[pl.ds(t0, CHUNK), (j0 + r) * LANES:(j0 + r + 1) * LANES]
            seg = cs_col - src[j0 + r:j0 + r + 1, :]
            ms.append((cb * jnp.exp2(jnp.where(causal, seg, NEG_BIG))).astype(BF16))
            bts.append(bt * w_state[j0 + r:j0 + r + 1, :])
            w_out.append(jnp.exp2(cs_col))
        lhs = jnp.concatenate([jnp.concatenate(ms, axis=1), jnp.concatenate(bts, axis=1)], axis=0)
        xbd = xbd_s[pl.ds(pl.multiple_of(c * (nh * CHUNK), nh * CHUNK), nh * CHUNK), :]
        both = jnp.dot(lhs, xbd, preferred_element_type=F32)
        first = col_i < SSD_HEAD_DIM
        w_out_x = jnp.concatenate([jnp.where(first, w_out[0], w_out[1]), jnp.where(first, w_out[2], w_out[3])], axis=1)
        prev = st_s[direction]
        y = both[:CHUNK] + jnp.dot(ck, prev.astype(BF16), preferred_element_type=F32) * w_out_x
        decay_x = _expand_heads_row([decay[j0 + r:j0 + r + 1, :] for r in range(nh)])
        st_s[direction] = prev * decay_x + both[CHUNK:]
        y_s[pl.ds(t0, CHUNK), :] = y_s[pl.ds(t0, CHUNK), :] + y

    def scan_body(i, carry):
        scan_step(0, i)
        scan_step(1, nchunk - 1 - i)
        return carry

    lax.fori_loop(0, nchunk, scan_body, 0, unroll=4)

    def fin(c, carry):
        t0 = pl.multiple_of(c * CHUNK, CHUNK)
        y = y_s[pl.ds(t0, CHUNK), :] * _silu(z_ref[pl.ds(t0, CHUNK), :].astype(F32))
        y = y * lax.rsqrt(jnp.mean(y * y, axis=-1, keepdims=True) + RMS_EPS)
        o_ref[pl.ds(t0, CHUNK), :] = (y * nw_ref[...]).astype(o_ref.dtype)
        return carry

    lax.fori_loop(0, nchunk, fin, 0, unroll=4)


def ssd_call(pf, col0, dtr, cw, cb, pa, pb, dsk, nw):
    bsz, s, _ = pf.shape
    g_ = SSD_GROUPS
    nh = HEADS_PER_GROUP
    xo, bo = col0 // GROUP_W, (col0 + D_INNER) // D_STATE
    co, zo = (col0 + D_INNER + g_ * D_STATE) // D_STATE, (col0 + D_INNER + 2 * g_ * D_STATE) // GROUP_W
    cwx, cwb, cwc = cw[:, :D_INNER], cw[:, D_INNER:D_INNER + g_ * D_STATE], cw[:, D_INNER + g_ * D_STATE:]
    cbx, cbb, cbc = cb[:, :D_INNER], cb[:, D_INNER:D_INNER + g_ * D_STATE], cb[:, D_INNER + g_ * D_STATE:]
    wide = lambda off: pl.BlockSpec((None, s, GROUP_W), lambda b, g: (b, 0, off + g))
    narrow = lambda off: pl.BlockSpec((None, s, D_STATE), lambda b, g: (b, 0, off + g))
    pw = lambda rows, width: pl.BlockSpec((rows, width), lambda b, g: (0, g))
    spread = np.zeros((LANES, 2 * nh * LANES), np.float32)
    for p in range(3):
        for j in range(2 * nh):
            spread[2 * nh * p + j, j * LANES:(j + 1) * LANES] = 1.0
    spread = jnp.asarray(spread, BF16)
    return pl.pallas_call(
        functools.partial(_ssd_kernel, seq=s),
        grid=(bsz, g_),
        in_specs=[wide(xo), narrow(bo), narrow(co), wide(zo),
                  pl.BlockSpec((None, 2 * nh, s), lambda b, g: (b, g, 0)),
                  pw(CONV_K, GROUP_W), pw(CONV_K, D_STATE), pw(CONV_K, D_STATE),
                  pw(1, GROUP_W), pw(1, D_STATE), pw(1, D_STATE),
                  pl.BlockSpec((None, 2 * nh, LANES), lambda b, g: (g, 0, 0)),
                  pl.BlockSpec((None, 2 * nh, LANES), lambda b, g: (g, 0, 0)),
                  pw(1, GROUP_W), pw(1, GROUP_W),
                  pl.BlockSpec(spread.shape, lambda b, g: (0, 0))],
        out_specs=pl.BlockSpec((None, s, GROUP_W), lambda b, g: (b, 0, g)),
        out_shape=jax.ShapeDtypeStruct((bsz, s, D_INNER), BF16),
        scratch_shapes=[pltpu.VMEM((s + 16, GROUP_W), F32),
                        pltpu.VMEM((s + 16, D_STATE), F32),
                        pltpu.VMEM((s + 16, D_STATE), F32),
                        pltpu.VMEM((nh * s, GROUP_W), BF16),
                        pltpu.VMEM((s, D_STATE), BF16),
                        pltpu.VMEM((s, D_STATE), BF16),
                        pltpu.VMEM((D_STATE, s), BF16),
                        pltpu.VMEM((s, CHUNK), F32),
                        pltpu.VMEM((6, 2 * nh, s), F32),
                        pltpu.VMEM((s, 2 * nh * LANES), F32),
                        pltpu.VMEM((s, GROUP_W), F32),
                        pltpu.VMEM((2, D_STATE, GROUP_W), F32)],
        compiler_params=_cparams("parallel", "parallel"),
        name="ssd",
    )(pf, pf, pf, pf, dtr, cwx, cwb, cwc, cbx, cbb, cbc, pa, pb, dsk, nw, spread)


def _mix_ffn_kernel(att_ref, yn_ref, g0_ref, g1_ref, bg0_ref, bg1_ref, h_ref, wa_ref, ws_ref, wo_ref,
                    l1g_ref, l1b_ref, w1_ref, w2_ref, l2g_ref, l2b_ref, of_ref, ob_ref, *, nsplit):
    ya = jnp.dot(att_ref[...], wa_ref[...], preferred_element_type=F32)
    ys = jnp.dot(yn_ref[...], ws_ref[...], preferred_element_type=F32)
    merged = (jax.nn.sigmoid(g0_ref[...].astype(F32) + bg0_ref[...]) * ya
              + jax.nn.sigmoid(g1_ref[...].astype(F32) + bg1_ref[...]) * ys)
    mix = jnp.dot(merged.astype(BF16), wo_ref[...], preferred_element_type=F32)
    h1 = _ln_rows(DN_ALPHA * h_ref[...] + mix, l1g_ref[...], l1b_ref[...])
    hb = h1.astype(BF16)
    step = w1_ref.shape[1] // nsplit
    acc = DN_ALPHA * h1
    for j in range(nsplit):
        a = jnp.dot(hb, w1_ref[:, j * step:(j + 1) * step], preferred_element_type=F32)
        a = jnp.square(jnp.maximum(a, 0.0)).astype(BF16)
        acc = acc + jnp.dot(a, w2_ref[j * step:(j + 1) * step, :], preferred_element_type=F32)
    y = _ln_rows(acc, l2g_ref[...], l2b_ref[...])
    of_ref[...] = y
    ob_ref[...] = y.astype(BF16)


def mix_ffn_call(att, yn, pf, gate_col, b_gate, h, wa, ws, wo, l1g, l1b, w1, w2, l2g, l2b, tm=512, nsplit=4):
    t, d = h.shape
    gate_off = gate_col // d
    row = lambda width: pl.BlockSpec((tm, width), lambda i: (i, 0))
    const = lambda a: pl.BlockSpec(a.shape, lambda i: (0,) * a.ndim, pipeline_mode=pl.Buffered(1))
    vec = pl.BlockSpec((1, d), lambda i: (0, 0))
    bg = b_gate.reshape(1, 2 * d)
    return pl.pallas_call(
        functools.partial(_mix_ffn_kernel, nsplit=nsplit),
        grid=(t // tm,),
        in_specs=[row(d), row(yn.shape[1]),
                  pl.BlockSpec((tm, d), lambda i: (i, gate_off)),
                  pl.BlockSpec((tm, d), lambda i: (i, gate_off + 1)),
                  pl.BlockSpec((1, d), lambda i: (0, 0)), pl.BlockSpec((1, d), lambda i: (0, 1)),
                  row(d), const(wa), const(ws), const(wo), vec, vec, const(w1), const(w2), vec, vec],
        out_specs=[row(d), row(d)],
        out_shape=[jax.ShapeDtypeStruct((t, d), F32), jax.ShapeDtypeStruct((t, d), BF16)],
        compiler_params=_cparams("parallel"),
        name="mix_ffn",
    )(att, yn, pf, pf, bg, bg, h, wa, ws, wo, l1g.reshape(1, d), l1b.reshape(1, d), w1, w2,
      l2g.reshape(1, d), l2b.reshape(1, d))


def _split_w_in(w):
    o = 3 * ATTN_W
    conv_ch = D_INNER + 2 * SSD_GROUPS * D_STATE
    w_qkv = w[:, :o]
    w_z = w[:, o:o + D_INNER]
    w_xbc = w[:, o + D_INNER:o + D_INNER + conv_ch]
    w_dt = w[:, o + D_INNER + conv_ch:o + D_INNER + conv_ch + 2 * SSD_HEADS]
    w_gate = w[:, o + D_INNER + conv_ch + 2 * SSD_HEADS:]
    w_dt = w_dt.reshape(-1, 2, SSD_GROUPS, HEADS_PER_GROUP).transpose(2, 1, 3, 0).reshape(2 * SSD_HEADS, -1)
    w_dt = jnp.pad(w_dt, ((0, LANES - 2 * SSD_HEADS), (0, 0)))
    return jnp.concatenate([w_qkv, w_xbc, w_z, w_gate], axis=1).astype(BF16), w_dt.astype(BF16)


def _group_rows(p):
    p = p.astype(F32).reshape(2, SSD_GROUPS, HEADS_PER_GROUP).transpose(1, 0, 2).reshape(SSD_GROUPS, 2 * HEADS_PER_GROUP)
    return jnp.broadcast_to(p[:, :, None], (SSD_GROUPS, 2 * HEADS_PER_GROUP, LANES))


def _layer(h, hb, bsz, s, w_in, conv_w, conv_b, a_log, dt_bias, d_skip, ssd_norm_w, rpb,
           w_attn_br, w_ssd_br, b_gate, w_o, ln1_g, ln1_b, w_ff1, w_ff2, ln2_g, ln2_b):
    t = bsz * s
    w_all, w_dt = _split_w_in(w_in)
    ssd_col = 3 * ATTN_W
    gate_col = ssd_col + 2 * D_INNER + 2 * SSD_GROUPS * D_STATE
    pf = proj_call(hb, w_all, BF16, 1024, 1024, "proj")
    dtr = proj_t_call(hb, w_dt, bsz, "proj_dt")

    att = natten_call(pf.reshape(bsz, s, -1), build_bias_blocks(rpb))

    pa = _group_rows(-jnp.exp(a_log.astype(F32)))
    pb = _group_rows(dt_bias)
    dsk = jnp.repeat(d_skip.astype(F32), SSD_HEAD_DIM).reshape(1, D_INNER)
    yn = ssd_call(pf.reshape(bsz, s, -1), ssd_col, dtr, conv_w.astype(F32), conv_b.astype(F32).reshape(1, -1),
                  pa, pb, dsk, ssd_norm_w.astype(F32).reshape(1, D_INNER))

    return mix_ffn_call(att.reshape(t, ATTN_W), yn.reshape(t, D_INNER), pf, gate_col, b_gate.astype(F32), h,
                        w_attn_br.astype(BF16), w_ssd_br.astype(BF16), w_o.astype(BF16), ln1_g, ln1_b,
                        w_ff1.astype(BF16), w_ff2.astype(BF16), ln2_g, ln2_b)


def kernel(x, ln0_g, ln0_b, w_in, conv_w, conv_b, a_log, dt_bias, d_skip, ssd_norm_w, rpb, w_attn_br, w_ssd_br,
           b_gate, w_o, ln1_g, ln1_b, w_ff1, w_ff2, ln2_g, ln2_b):
    bsz, s, d = x.shape
    h, hb = layer_norm_call(x.reshape(bsz * s, d), ln0_g, ln0_b)
    for l in range(w_in.shape[0]):
        h, hb = _layer(h, hb, bsz, s, w_in[l], conv_w[l], conv_b[l], a_log[l], dt_bias[l], d_skip[l],
                       ssd_norm_w[l], rpb[l], w_attn_br[l], w_ssd_br[l], b_gate[l], w_o[l], ln1_g[l], ln1_b[l],
                       w_ff1[l], w_ff2[l], ln2_g[l], ln2_b[l])
    return h.reshape(bsz, s, d)
```

```python
import functools
import math

import jax
import jax.numpy as jnp
import numpy as np
from jax import lax
from jax.experimental import pallas as pl
from jax.experimental.pallas import tpu as pltpu

D_MODEL = 1024
DEPTH = 2
GRID_W = 64
NA_HEADS = 16
NA_HEAD_DIM = 64
ATTN_W = NA_HEADS * NA_HEAD_DIM
WIN_R = 8
WIN_C = 16
D_INNER = 2048
SSD_HEAD_DIM = 64
SSD_HEADS = D_INNER // SSD_HEAD_DIM
SSD_GROUPS = 8
HEADS_PER_GROUP = SSD_HEADS // SSD_GROUPS
D_STATE = 128
CONV_K = 5
CHUNK = 128
GROUP_W = D_INNER // SSD_GROUPS
D_FF = 4 * D_MODEL
DN_ALPHA = (2 * DEPTH) ** 0.25
LN_EPS = 1e-5
LOG2E = math.log2(math.e)
RMS_EPS = 1e-5

V7X_VMEM_BYTES = 64 * 1024 * 1024
VMEM_LIMIT = 56 * 1024 * 1024
LANES = 128
NEG_BIG = -1e30

F32 = jnp.float32
BF16 = jnp.bfloat16


def _cparams(*sem):
    return pltpu.CompilerParams(dimension_semantics=sem, vmem_limit_bytes=VMEM_LIMIT)


def _ln_rows(x, g, b):
    mu = jnp.mean(x, axis=-1, keepdims=True)
    xc = x - mu
    var = jnp.mean(xc * xc, axis=-1, keepdims=True)
    return xc * lax.rsqrt(var + LN_EPS) * g + b


def _ln_kernel(x_ref, g_ref, b_ref, of_ref, ob_ref):
    y = _ln_rows(x_ref[...], g_ref[...], b_ref[...])
    of_ref[...] = y
    ob_ref[...] = y.astype(BF16)


def layer_norm_call(x, g, b, tm=512):
    t, d = x.shape
    return pl.pallas_call(
        _ln_kernel,
        grid=(t // tm,),
        in_specs=[pl.BlockSpec((tm, d), lambda i: (i, 0)),
                  pl.BlockSpec((1, d), lambda i: (0, 0)),
                  pl.BlockSpec((1, d), lambda i: (0, 0))],
        out_specs=[pl.BlockSpec((tm, d), lambda i: (i, 0)),
                   pl.BlockSpec((tm, d), lambda i: (i, 0))],
        out_shape=[jax.ShapeDtypeStruct((t, d), F32), jax.ShapeDtypeStruct((t, d), BF16)],
        compiler_params=_cparams("parallel"),
        name="ln0",
    )(x, g.reshape(1, d), b.reshape(1, d))


def _proj_t_kernel(x_ref, w_ref, o_ref):
    o_ref[...] = lax.dot_general(w_ref[...], x_ref[...], (((1,), (1,)), ((), ())), preferred_element_type=F32)


def proj_t_call(x, w_t, layer, bsz, name):
    t, k = x.shape
    s = t // bsz
    r = w_t.shape[1]
    return pl.pallas_call(
        _proj_t_kernel,
        grid=(bsz,),
        in_specs=[pl.BlockSpec((s, k), lambda b: (b, 0)),
                  pl.BlockSpec((None, r, k), lambda b: (layer, 0, 0))],
        out_specs=pl.BlockSpec((None, r, s), lambda b: (b, 0, 0)),
        out_shape=jax.ShapeDtypeStruct((bsz, r, s), F32),
        compiler_params=_cparams("parallel"),
        name=name,
    )(x, w_t)


def _proj_kernel(x_ref, w_ref, o_ref):
    tn = o_ref.shape[1]
    col = pl.multiple_of(pl.program_id(1) * tn, tn)
    o_ref[...] = jnp.dot(x_ref[...], w_ref[:, pl.ds(col, tn)], preferred_element_type=F32).astype(o_ref.dtype)


def proj_call(x, w, layer, out_dtype, tm, tn, name):
    t, k = x.shape
    n = w.shape[2]
    return pl.pallas_call(
        _proj_kernel,
        grid=(t // tm, n // tn),
        in_specs=[pl.BlockSpec((tm, k), lambda i, j: (i, 0)),
                  pl.BlockSpec((None, k, n), lambda i, j: (layer, 0, 0), pipeline_mode=pl.Buffered(1))],
        out_specs=pl.BlockSpec((tm, tn), lambda i, j: (i, j)),
        out_shape=jax.ShapeDtypeStruct((t, n), out_dtype),
        compiler_params=_cparams("parallel", "arbitrary"),
        name=name,
    )(x, w)


Q_GROUP = 4
K_SLAB = Q_GROUP + WIN_R


def _slab_start(j, rows):
    return jnp.clip(Q_GROUP * j - WIN_R // 2, 0, rows - K_SLAB)


def _group_patterns(rows):
    ngroups = rows // Q_GROUP
    outside = 2 * WIN_R - 1
    pats = []
    for j in range(ngroups):
        ws = min(max(Q_GROUP * j - WIN_R // 2, 0), rows - K_SLAB)
        pat = np.full((Q_GROUP, K_SLAB), outside, np.int64)
        for a in range(Q_GROUP):
            r = Q_GROUP * j + a
            r0 = min(max(r - WIN_R // 2, 0), rows - WIN_R)
            for i in range(K_SLAB):
                if r0 <= ws + i < r0 + WIN_R:
                    pat[a, i] = ws + i - r + WIN_R - 1
        pats.append(pat)
    assert all(np.array_equal(pats[1], p) for p in pats[1:-1])
    return [pats[0].tolist(), pats[1].tolist(), pats[-1].tolist()]


def _natten_kernel(q_ref, k_ref, v_ref, blk_ref, o_ref, bias_s, *, rows):
    ngroups = rows // Q_GROUP
    nq = Q_GROUP * GRID_W
    nk = K_SLAB * GRID_W
    lane = lax.broadcasted_iota(jnp.int32, (nq, LANES), 1)
    first = lane < NA_HEAD_DIM

    @pl.when(pl.program_id(1) == 0)
    def _():
        for v, pat in enumerate(_group_patterns(rows)):
            for hh in range(2):
                for a in range(Q_GROUP):
                    for i in range(0, K_SLAB, 2):
                        pair = jnp.concatenate([blk_ref[hh, pat[a][i]], blk_ref[hh, pat[a][i + 1]]], axis=1)
                        bias_s[v, hh, a * GRID_W:(a + 1) * GRID_W, i * GRID_W:(i + 2) * GRID_W] = pair

    def group_body(j, carry):
        variant = jnp.minimum(j, 1) + jnp.maximum(j - (ngroups - 2), 0)
        qs = pl.multiple_of(j * nq, nq)
        ks = pl.multiple_of(_slab_start(j, rows) * GRID_W, GRID_W)
        q = (q_ref[pl.ds(qs, nq), :].astype(F32) * (NA_HEAD_DIM ** -0.5 * LOG2E)).astype(BF16)
        kw = k_ref[pl.ds(ks, nk), :]
        vw = v_ref[pl.ds(ks, nk), :]
        outs = []
        for hh in range(2):
            keep = first if hh == 0 else jnp.logical_not(first)
            qh = jnp.where(keep, q, jnp.zeros_like(q))
            s = lax.dot_general(qh, kw, (((1,), (1,)), ((), ())), preferred_element_type=F32)
            s = s + bias_s[variant, hh]
            m = jnp.max(s, axis=-1, keepdims=True)
            p = jnp.exp2(s - m)
            l = jnp.sum(p, axis=-1, keepdims=True)
            o = jnp.dot(p.astype(BF16), vw, preferred_element_type=F32)
            outs.append(o / l)
        o_ref[pl.ds(qs, nq), :] = jnp.where(first, outs[0], outs[1]).astype(o_ref.dtype)
        return carry

    lax.fori_loop(0, ngroups, group_body, 0, unroll=4)


def natten_call(qkv, blocks):
    bsz, s, _ = qkv.shape
    rows = s // GRID_W
    assert rows % Q_GROUP == 0 and rows >= K_SLAB + Q_GROUP and K_SLAB % 2 == 0
    npairs = NA_HEADS // 2
    blk = lambda off: pl.BlockSpec((None, s, LANES), lambda hp, b: (b, 0, off + hp))
    return pl.pallas_call(
        functools.partial(_natten_kernel, rows=rows),
        grid=(npairs, bsz),
        in_specs=[blk(0), blk(npairs), blk(2 * npairs),
                  pl.BlockSpec((2,) + blocks.shape[1:], lambda hp, b: (hp, 0, 0, 0))],
        out_specs=pl.BlockSpec((None, s, LANES), lambda hp, b: (b, 0, hp)),
        out_shape=jax.ShapeDtypeStruct((bsz, s, ATTN_W), BF16),
        scratch_shapes=[pltpu.VMEM((3, 2, Q_GROUP * GRID_W, K_SLAB * GRID_W), F32)],
        compiler_params=_cparams("arbitrary", "arbitrary"),
        name="natten",
    )(qkv, qkv, qkv, blocks)


def build_bias_blocks(rpb):
    w = np.arange(GRID_W)[:, None]
    kc = np.arange(GRID_W)[None, :]
    c0 = np.clip(w - WIN_C // 2, 0, GRID_W - WIN_C)
    col_ok = (kc >= c0) & (kc < c0 + WIN_C)
    col_sel = np.zeros((GRID_W, GRID_W, 2 * WIN_C - 1), np.float32)
    col_sel[w, kc, np.clip(kc - w + WIN_C - 1, 0, 2 * WIN_C - 2)] = col_ok
    blocks = jnp.einsum("hrd,wkd->hrwk", rpb.astype(F32) * LOG2E, col_sel, precision=lax.Precision.HIGHEST)
    blocks = jnp.where(col_ok[None, None], blocks, NEG_BIG)
    return jnp.concatenate([blocks, jnp.full_like(blocks[:, :1], NEG_BIG)], axis=1)


def _silu(x):
    return x * jax.nn.sigmoid(x)


def _softplus(x):
    return jnp.maximum(x, 0.0) + jnp.log1p(jnp.exp(-jnp.abs(x)))


def _chunk_scan(x, reverse):
    width = x.shape[1]
    lane = lax.broadcasted_iota(jnp.int32, x.shape, 1) & (CHUNK - 1)
    sh = 1
    while sh < CHUNK:
        if reverse:
            x = x + jnp.where(lane < CHUNK - sh, pltpu.roll(x, width - sh, 1), 0.0)
        else:
            x = x + jnp.where(lane >= sh, pltpu.roll(x, sh, 1), 0.0)
        sh *= 2
    return x


def _expand_heads_row(rows):
    lane = lax.broadcasted_iota(jnp.int32, (1, LANES), 1)
    first = lane < SSD_HEAD_DIM
    return jnp.concatenate([jnp.where(first, rows[0], rows[1]), jnp.where(first, rows[2], rows[3])], axis=1)


def _ssd_kernel(xs_ref, bm_ref, cm_ref, z_ref, dtr_ref, cwx_ref, cwb_ref, cwc_ref, cbx_ref, cbb_ref, cbc_ref,
                pa_ref, pb_ref, dsk_ref, nw_ref, spread_ref, o_ref,
                pad_x, pad_b, pad_c, xbd_s, bc_s, cc_s, bt_s, cb_s, rows_s, csb_s, y_s, st_s,
                *, seq):
    nchunk = seq // CHUNK
    halo = 8
    nh = HEADS_PER_GROUP
    lane256 = lax.broadcasted_iota(jnp.int32, (CHUNK, GROUP_W), 1)
    head_masks = [(lane256 >= SSD_HEAD_DIM * r) & (lane256 < SSD_HEAD_DIM * (r + 1)) for r in range(nh)]

    a_rows = jnp.concatenate([pa_ref[...]] * nchunk, axis=1)
    bias_rows = jnp.concatenate([pb_ref[...]] * nchunk, axis=1)
    is_bwd = lax.broadcasted_iota(jnp.int32, (2 * nh, seq), 0) >= nh
    dt = _softplus(dtr_ref[...] + bias_rows)
    adt = dt * a_rows
    pre = _chunk_scan(adt, reverse=False)
    suf = _chunk_scan(adt, reverse=True)
    cs = jnp.where(is_bwd, suf, pre)
    total = pre + suf - adt
    rows_s[0] = (cs - jnp.log(dt)) * LOG2E
    rows_s[1] = jnp.exp(total)
    rows_s[2] = dt * jnp.exp(total - cs)
    cs2 = cs * LOG2E
    hi = cs2.astype(BF16).astype(F32)
    mid = (cs2 - hi).astype(BF16).astype(F32)
    rows_s[3] = hi
    rows_s[4] = mid
    rows_s[5] = ((cs2 - hi) - mid).astype(BF16).astype(F32)

    def pad_copy(src_ref, pad_ref):
        width = src_ref.shape[-1]
        pad_ref[pl.ds(0, halo), :] = jnp.zeros((halo, width), F32)
        pad_ref[pl.ds(halo + seq, halo), :] = jnp.zeros((halo, width), F32)
        pad_ref[pl.ds(halo, seq), :] = src_ref[...].astype(F32)

    def conv_chunk(pad_ref, w_ref, b_ref, t0):
        acc = jnp.broadcast_to(b_ref[...], (CHUNK, pad_ref.shape[-1]))
        for k in range(CONV_K):
            acc = acc + pad_ref[pl.ds(t0 + halo + k - CONV_K // 2, CHUNK), :] * w_ref[k:k + 1, :]
        return _silu(acc)

    pad_copy(bm_ref, pad_b)
    pad_copy(cm_ref, pad_c)
    pad_copy(xs_ref, pad_x)
    for c in range(nchunk):
        t0 = c * CHUNK
        vb = conv_chunk(pad_b, cwb_ref, cbb_ref, t0)
        vc = conv_chunk(pad_c, cwc_ref, cbc_ref, t0).astype(BF16)
        vx = conv_chunk(pad_x, cwx_ref, cbx_ref, t0)
        bc_s[pl.ds(t0, CHUNK), :] = vb.astype(BF16)
        bt_s[:, pl.ds(t0, CHUNK)] = vb.T.astype(BF16)
        cc_s[pl.ds(t0, CHUNK), :] = vc
        y_s[pl.ds(t0, CHUNK), :] = vx * dsk_ref[...]
        vxb = vx.astype(BF16)
        for r in range(nh):
            xbd_s[pl.ds((c * nh + r) * CHUNK, CHUNK), :] = jnp.where(head_masks[r], vxb, jnp.zeros_like(vxb))
        cb_s[pl.ds(t0, CHUNK), :] = lax.dot_general(vc, vb.astype(BF16), (((1,), (1,)), ((), ())),
                                                    preferred_element_type=F32)
        pieces = jnp.concatenate([rows_s[3, :, pl.ds(t0, CHUNK)], rows_s[4, :, pl.ds(t0, CHUNK)],
                                  rows_s[5, :, pl.ds(t0, CHUNK)], jnp.zeros((CHUNK - 6 * nh, CHUNK), F32)], axis=0)
        csb_s[pl.ds(t0, CHUNK), :] = jnp.dot(pieces.T.astype(BF16), spread_ref[...], preferred_element_type=F32)

    row_i = lax.broadcasted_iota(jnp.int32, (CHUNK, CHUNK), 0)
    col_i = lax.broadcasted_iota(jnp.int32, (CHUNK, CHUNK), 1)
    st_s[...] = jnp.zeros(st_s.shape, F32)

    def scan_step(direction, c):
        causal = (row_i >= col_i) if direction == 0 else (row_i <= col_i)
        t0 = pl.multiple_of(c * CHUNK, CHUNK)
        ck = cc_s[pl.ds(t0, CHUNK), :]
        cb = cb_s[pl.ds(t0, CHUNK), :]
        bt = bt_s[:, pl.ds(t0, CHUNK)]
        src = rows_s[0, :, pl.ds(t0, CHUNK)]
        decay = rows_s[1, :, pl.ds(t0, CHUNK)]
        w_state = rows_s[2, :, pl.ds(t0, CHUNK)].astype(BF16)
        j0 = direction * nh
        ms, bts, w_out = [], [], []
        for r in range(nh):
            cs_col = csb_s[pl.ds(t0, CHUNK), (j0 + r) * LANES:(j0 + r + 1) * LANES]
            seg = cs_col - src[j0 + r:j0 + r + 1, :]
            ms.append((cb * jnp.exp2(jnp.where(causal, seg, NEG_BIG))).astype(BF16))
            bts.append(bt * w_state[j0 + r:j0 + r + 1, :])
            w_out.append(jnp.exp2(cs_col))
        lhs = jnp.concatenate([jnp.concatenate(ms, axis=1), jnp.concatenate(bts, axis=1)], axis=0)
        xbd = xbd_s[pl.ds(pl.multiple_of(c * (nh * CHUNK), nh * CHUNK), nh * CHUNK), :]
        both = jnp.dot(lhs, xbd, preferred_element_type=F32)
        first = col_i < SSD_HEAD_DIM
        w_out_x = jnp.concatenate([jnp.where(first, w_out[0], w_out[1]), jnp.where(first, w_out[2], w_out[3])], axis=1)
        prev = st_s[direction]
        y = both[:CHUNK] + jnp.dot(ck, prev.astype(BF16), preferred_element_type=F32) * w_out_x
        decay_x = _expand_heads_row([decay[j0 + r:j0 + r + 1, :] for r in range(nh)])
        st_s[direction] = prev * decay_x + both[CHUNK:]
        y_s[pl.ds(t0, CHUNK), :] = y_s[pl.ds(t0, CHUNK), :] + y

    def scan_body(i, carry):
        scan_step(0, i)
        scan_step(1, nchunk - 1 - i)
        return carry

    lax.fori_loop(0, nchunk, scan_body, 0, unroll=4)

    def fin(c, carry):
        t0 = pl.multiple_of(c * CHUNK, CHUNK)
        y = y_s[pl.ds(t0, CHUNK), :] * _silu(z_ref[pl.ds(t0, CHUNK), :].astype(F32))
        y = y * lax.rsqrt(jnp.mean(y * y, axis=-1, keepdims=True) + RMS_EPS)
        o_ref[pl.ds(t0, CHUNK), :] = (y * nw_ref[...]).astype(o_ref.dtype)
        return carry

    lax.fori_loop(0, nchunk, fin, 0, unroll=4)


def ssd_call(pf, col0, dtr, cw, cb, pa, pb, dsk, nw):
    bsz, s, _ = pf.shape
    g_ = SSD_GROUPS
    nh = HEADS_PER_GROUP
    xo, bo = col0 // GROUP_W, (col0 + D_INNER) // D_STATE
    co, zo = (col0 + D_INNER + g_ * D_STATE) // D_STATE, (col0 + D_INNER + 2 * g_ * D_STATE) // GROUP_W
    cwx, cwb, cwc = cw[:, :D_INNER], cw[:, D_INNER:D_INNER + g_ * D_STATE], cw[:, D_INNER + g_ * D_STATE:]
    cbx, cbb, cbc = cb[:, :D_INNER], cb[:, D_INNER:D_INNER + g_ * D_STATE], cb[:, D_INNER + g_ * D_STATE:]
    wide = lambda off: pl.BlockSpec((None, s, GROUP_W), lambda b, g: (b, 0, off + g))
    narrow = lambda off: pl.BlockSpec((None, s, D_STATE), lambda b, g: (b, 0, off + g))
    pw = lambda rows, width: pl.BlockSpec((rows, width), lambda b, g: (0, g))
    spread = np.zeros((LANES, 2 * nh * LANES), np.float32)
    for p in range(3):
        for j in range(2 * nh):
            spread[2 * nh * p + j, j * LANES:(j + 1) * LANES] = 1.0
    spread = jnp.asarray(spread, BF16)
    return pl.pallas_call(
        functools.partial(_ssd_kernel, seq=s),
        grid=(bsz, g_),
        in_specs=[wide(xo), narrow(bo), narrow(co), wide(zo),
                  pl.BlockSpec((None, 2 * nh, s), lambda b, g: (b, g, 0)),
                  pw(CONV_K, GROUP_W), pw(CONV_K, D_STATE), pw(CONV_K, D_STATE),
                  pw(1, GROUP_W), pw(1, D_STATE), pw(1, D_STATE),
                  pl.BlockSpec((None, 2 * nh, LANES), lambda b, g: (g, 0, 0)),
                  pl.BlockSpec((None, 2 * nh, LANES), lambda b, g: (g, 0, 0)),
                  pw(1, GROUP_W), pw(1, GROUP_W),
                  pl.BlockSpec(spread.shape, lambda b, g: (0, 0))],
        out_specs=pl.BlockSpec((None, s, GROUP_W), lambda b, g: (b, 0, g)),
        out_shape=jax.ShapeDtypeStruct((bsz, s, D_INNER), BF16),
        scratch_shapes=[pltpu.VMEM((s + 16, GROUP_W), F32),
                        pltpu.VMEM((s + 16, D_STATE), F32),
                        pltpu.VMEM((s + 16, D_STATE), F32),
                        pltpu.VMEM((nh * s, GROUP_W), BF16),
                        pltpu.VMEM((s, D_STATE), BF16),
                        pltpu.VMEM((s, D_STATE), BF16),
                        pltpu.VMEM((D_STATE, s), BF16),
                        pltpu.VMEM((s, CHUNK), F32),
                        pltpu.VMEM((6, 2 * nh, s), F32),
                        pltpu.VMEM((s, 2 * nh * LANES), F32),
                        pltpu.VMEM((s, GROUP_W), F32),
                        pltpu.VMEM((2, D_STATE, GROUP_W), F32)],
        compiler_params=_cparams("parallel", "parallel"),
        name="ssd",
    )(pf, pf, pf, pf, dtr, cwx, cwb, cwc, cbx, cbb, cbc, pa, pb, dsk, nw, spread)


def _mix_ffn_kernel(att_ref, yn_ref, g0_ref, g1_ref, bg0_ref, bg1_ref, h_ref, wa_ref, ws_ref, wo_ref,
                    l1g_ref, l1b_ref, w1_ref, w2_ref, l2g_ref, l2b_ref, of_ref, ob_ref, *, nsplit):
    ya = jnp.dot(att_ref[...], wa_ref[...], preferred_element_type=F32)
    ys = jnp.dot(yn_ref[...], ws_ref[...], preferred_element_type=F32)
    merged = (jax.nn.sigmoid(g0_ref[...].astype(F32) + bg0_ref[...]) * ya
              + jax.nn.sigmoid(g1_ref[...].astype(F32) + bg1_ref[...]) * ys)
    mix = jnp.dot(merged.astype(BF16), wo_ref[...], preferred_element_type=F32)
    h1 = _ln_rows(DN_ALPHA * h_ref[...] + mix, l1g_ref[...], l1b_ref[...])
    hb = h1.astype(BF16)
    step = w1_ref.shape[1] // nsplit
    acc = DN_ALPHA * h1
    for j in range(nsplit):
        a = jnp.dot(hb, w1_ref[:, j * step:(j + 1) * step], preferred_element_type=F32)
        a = jnp.square(jnp.maximum(a, 0.0)).astype(BF16)
        acc = acc + jnp.dot(a, w2_ref[j * step:(j + 1) * step, :], preferred_element_type=F32)
    y = _ln_rows(acc, l2g_ref[...], l2b_ref[...])
    of_ref[...] = y
    ob_ref[...] = y.astype(BF16)


def mix_ffn_call(att, yn, pf, gate_col, b_gate, h, layer, wa, ws, wo, l1g, l1b, w1, w2, l2g, l2b, tm=512, nsplit=4):
    t, d = h.shape
    gate_off = gate_col // d
    row = lambda width: pl.BlockSpec((tm, width), lambda i: (i, 0))
    const = lambda a: pl.BlockSpec((None,) + a.shape[1:], lambda i: (layer, 0, 0), pipeline_mode=pl.Buffered(1))
    vec = pl.BlockSpec((1, d), lambda i: (0, 0))
    bg = b_gate.reshape(1, 2 * d)
    return pl.pallas_call(
        functools.partial(_mix_ffn_kernel, nsplit=nsplit),
        grid=(t // tm,),
        in_specs=[row(d), row(yn.shape[1]),
                  pl.BlockSpec((tm, d), lambda i: (i, gate_off)),
                  pl.BlockSpec((tm, d), lambda i: (i, gate_off + 1)),
                  pl.BlockSpec((1, d), lambda i: (0, 0)), pl.BlockSpec((1, d), lambda i: (0, 1)),
                  row(d), const(wa), const(ws), const(wo), vec, vec, const(w1), const(w2), vec, vec],
        out_specs=[row(d), row(d)],
        out_shape=[jax.ShapeDtypeStruct((t, d), F32), jax.ShapeDtypeStruct((t, d), BF16)],
        compiler_params=_cparams("parallel"),
        name="mix_ffn",
    )(att, yn, pf, pf, bg, bg, h, wa, ws, wo, l1g.reshape(1, d), l1b.reshape(1, d), w1, w2,
      l2g.reshape(1, d), l2b.reshape(1, d))


def _split_w_in(w):
    o = 3 * ATTN_W
    conv_ch = D_INNER + 2 * SSD_GROUPS * D_STATE
    nl, d = w.shape[0], w.shape[1]
    w_qkv = w[:, :, :o]
    w_z = w[:, :, o:o + D_INNER]
    w_xbc = w[:, :, o + D_INNER:o + D_INNER + conv_ch]
    w_dt = w[:, :, o + D_INNER + conv_ch:o + D_INNER + conv_ch + 2 * SSD_HEADS]
    w_gate = w[:, :, o + D_INNER + conv_ch + 2 * SSD_HEADS:]
    w_dt = w_dt.reshape(nl, d, 2, SSD_GROUPS, HEADS_PER_GROUP).transpose(0, 3, 2, 4, 1).reshape(nl, 2 * SSD_HEADS, d)
    w_dt = jnp.pad(w_dt, ((0, 0), (0, LANES - 2 * SSD_HEADS), (0, 0)))
    return jnp.concatenate([w_qkv, w_xbc, w_z, w_gate], axis=2).astype(BF16), w_dt.astype(BF16)


def _group_rows(p):
    p = p.astype(F32).reshape(2, SSD_GROUPS, HEADS_PER_GROUP).transpose(1, 0, 2).reshape(SSD_GROUPS, 2 * HEADS_PER_GROUP)
    return jnp.broadcast_to(p[:, :, None], (SSD_GROUPS, 2 * HEADS_PER_GROUP, LANES))


def _layer(h, hb, bsz, s, layer, w_all, w_dt, conv_w, conv_b, a_log, dt_bias, d_skip, ssd_norm_w, rpb,
           wa, ws, b_gate, wo, ln1_g, ln1_b, w1, w2, ln2_g, ln2_b):
    t = bsz * s
    ssd_col = 3 * ATTN_W
    gate_col = ssd_col + 2 * D_INNER + 2 * SSD_GROUPS * D_STATE
    pf = proj_call(hb, w_all, layer, BF16, 1024, 1024, "proj")
    dtr = proj_t_call(hb, w_dt, layer, bsz, "proj_dt")

    att = natten_call(pf.reshape(bsz, s, -1), build_bias_blocks(rpb))

    pa = _group_rows(-jnp.exp(a_log.astype(F32)))
    pb = _group_rows(dt_bias)
    dsk = jnp.repeat(d_skip.astype(F32), SSD_HEAD_DIM).reshape(1, D_INNER)
    yn = ssd_call(pf.reshape(bsz, s, -1), ssd_col, dtr, conv_w.astype(F32), conv_b.astype(F32).reshape(1, -1),
                  pa, pb, dsk, ssd_norm_w.astype(F32).reshape(1, D_INNER))

    return mix_ffn_call(att.reshape(t, ATTN_W), yn.reshape(t, D_INNER), pf, gate_col, b_gate.astype(F32), h,
                        layer, wa, ws, wo, ln1_g, ln1_b, w1, w2, ln2_g, ln2_b)


def kernel(x, ln0_g, ln0_b, w_in, conv_w, conv_b, a_log, dt_bias, d_skip, ssd_norm_w, rpb, w_attn_br, w_ssd_br,
           b_gate, w_o, ln1_g, ln1_b, w_ff1, w_ff2, ln2_g, ln2_b):
    bsz, s, d = x.shape
    h, hb = layer_norm_call(x.reshape(bsz * s, d), ln0_g, ln0_b)
    w_all, w_dt = _split_w_in(w_in)
    wa, ws, wo = w_attn_br.astype(BF16), w_ssd_br.astype(BF16), w_o.astype(BF16)
    w1, w2 = w_ff1.astype(BF16), w_ff2.astype(BF16)
    for l in range(w_in.shape[0]):
        h, hb = _layer(h, hb, bsz, s, l, w_all, w_dt, conv_w[l], conv_b[l], a_log[l], dt_bias[l], d_skip[l],
                       ssd_norm_w[l], rpb[l], wa, ws, b_gate[l], wo, ln1_g[l], ln1_b[l], w1, w2, ln2_g[l], ln2_b[l])
    return h.reshape(bsz, s, d)
```

```python
import functools
import math

import jax
import jax.numpy as jnp
import numpy as np
from jax import lax
from jax.experimental import pallas as pl
from jax.experimental.pallas import tpu as pltpu

D_MODEL = 1024
DEPTH = 2
GRID_W = 64
NA_HEADS = 16
NA_HEAD_DIM = 64
ATTN_W = NA_HEADS * NA_HEAD_DIM
WIN_R = 8
WIN_C = 16
D_INNER = 2048
SSD_HEAD_DIM = 64
SSD_HEADS = D_INNER // SSD_HEAD_DIM
SSD_GROUPS = 8
HEADS_PER_GROUP = SSD_HEADS // SSD_GROUPS
D_STATE = 128
CONV_K = 5
CHUNK = 128
GROUP_W = D_INNER // SSD_GROUPS
D_FF = 4 * D_MODEL
DN_ALPHA = (2 * DEPTH) ** 0.25
LN_EPS = 1e-5
LOG2E = math.log2(math.e)
RMS_EPS = 1e-5

V7X_VMEM_BYTES = 64 * 1024 * 1024
VMEM_LIMIT = 56 * 1024 * 1024
LANES = 128
NEG_BIG = -1e30

F32 = jnp.float32
BF16 = jnp.bfloat16


def _cparams(*sem):
    return pltpu.CompilerParams(dimension_semantics=sem, vmem_limit_bytes=VMEM_LIMIT)


def _ln_rows(x, g, b):
    mu = jnp.mean(x, axis=-1, keepdims=True)
    xc = x - mu
    var = jnp.mean(xc * xc, axis=-1, keepdims=True)
    return xc * lax.rsqrt(var + LN_EPS) * g + b


def _ln_kernel(x_ref, g_ref, b_ref, of_ref, ob_ref):
    y = _ln_rows(x_ref[...], g_ref[...], b_ref[...])
    of_ref[...] = y
    ob_ref[...] = y.astype(BF16)


def layer_norm_call(x, g, b, tm=512):
    t, d = x.shape
    return pl.pallas_call(
        _ln_kernel,
        grid=(t // tm,),
        in_specs=[pl.BlockSpec((tm, d), lambda i: (i, 0)),
                  pl.BlockSpec((1, d), lambda i: (0, 0)),
                  pl.BlockSpec((1, d), lambda i: (0, 0))],
        out_specs=[pl.BlockSpec((tm, d), lambda i: (i, 0)),
                   pl.BlockSpec((tm, d), lambda i: (i, 0))],
        out_shape=[jax.ShapeDtypeStruct((t, d), F32), jax.ShapeDtypeStruct((t, d), BF16)],
        compiler_params=_cparams("parallel"),
        name="ln0",
    )(x, g.reshape(1, d), b.reshape(1, d))


def _proj_t_kernel(x_ref, w_ref, o_ref):
    o_ref[...] = lax.dot_general(w_ref[...], x_ref[...], (((1,), (1,)), ((), ())), preferred_element_type=F32)


def proj_t_call(x, w_t, layer, bsz, name):
    t, k = x.shape
    s = t // bsz
    r = w_t.shape[1]
    return pl.pallas_call(
        _proj_t_kernel,
        grid=(bsz,),
        in_specs=[pl.BlockSpec((s, k), lambda b: (b, 0)),
                  pl.BlockSpec((None, r, k), lambda b: (layer, 0, 0))],
        out_specs=pl.BlockSpec((None, r, s), lambda b: (b, 0, 0)),
        out_shape=jax.ShapeDtypeStruct((bsz, r, s), F32),
        compiler_params=_cparams("parallel"),
        name=name,
    )(x, w_t)


def _proj_kernel(x_ref, w_ref, o_ref, *, row_starts):
    tn = o_ref.shape[1]
    j = pl.program_id(1)
    row = row_starts[0]
    for idx in range(1, len(row_starts)):
        row = jnp.where(j >= idx, row_starts[idx], row)
    w = w_ref[pl.ds(pl.multiple_of(row, 64), tn), :]
    o_ref[...] = lax.dot_general(x_ref[...], w, (((1,), (1,)), ((), ())),
                                 preferred_element_type=F32).astype(o_ref.dtype)


def proj_call(x, w_t, layer, row_starts, out_dtype, tm, tn, name):
    t, k = x.shape
    n_rows = w_t.shape[1]
    n = tn * len(row_starts)
    return pl.pallas_call(
        functools.partial(_proj_kernel, row_starts=tuple(row_starts)),
        grid=(t // tm, len(row_starts)),
        in_specs=[pl.BlockSpec((tm, k), lambda i, j: (i, 0)),
                  pl.BlockSpec((None, n_rows, k), lambda i, j: (layer, 0, 0), pipeline_mode=pl.Buffered(1))],
        out_specs=pl.BlockSpec((tm, tn), lambda i, j: (i, j)),
        out_shape=jax.ShapeDtypeStruct((t, n), out_dtype),
        compiler_params=_cparams("parallel", "arbitrary"),
        name=name,
    )(x, w_t)


Q_GROUP = 4
K_SLAB = Q_GROUP + WIN_R


def _slab_start(j, rows):
    return jnp.clip(Q_GROUP * j - WIN_R // 2, 0, rows - K_SLAB)


def _group_patterns(rows):
    ngroups = rows // Q_GROUP
    outside = 2 * WIN_R - 1
    pats = []
    for j in range(ngroups):
        ws = min(max(Q_GROUP * j - WIN_R // 2, 0), rows - K_SLAB)
        pat = np.full((Q_GROUP, K_SLAB), outside, np.int64)
        for a in range(Q_GROUP):
            r = Q_GROUP * j + a
            r0 = min(max(r - WIN_R // 2, 0), rows - WIN_R)
            for i in range(K_SLAB):
                if r0 <= ws + i < r0 + WIN_R:
                    pat[a, i] = ws + i - r + WIN_R - 1
        pats.append(pat)
    assert all(np.array_equal(pats[1], p) for p in pats[1:-1])
    return [pats[0].tolist(), pats[1].tolist(), pats[-1].tolist()]


def _natten_kernel(q_ref, k_ref, v_ref, blk_ref, o_ref, bias_s, *, rows):
    ngroups = rows // Q_GROUP
    nq = Q_GROUP * GRID_W
    nk = K_SLAB * GRID_W
    lane = lax.broadcasted_iota(jnp.int32, (nq, LANES), 1)
    first = lane < NA_HEAD_DIM

    @pl.when(pl.program_id(1) == 0)
    def _():
        for v, pat in enumerate(_group_patterns(rows)):
            for hh in range(2):
                for a in range(Q_GROUP):
                    for i in range(0, K_SLAB, 2):
                        pair = jnp.concatenate([blk_ref[hh, pat[a][i]], blk_ref[hh, pat[a][i + 1]]], axis=1)
                        bias_s[v, hh, a * GRID_W:(a + 1) * GRID_W, i * GRID_W:(i + 2) * GRID_W] = pair

    def group_body(j, carry):
        variant = jnp.minimum(j, 1) + jnp.maximum(j - (ngroups - 2), 0)
        qs = pl.multiple_of(j * nq, nq)
        ks = pl.multiple_of(_slab_start(j, rows) * GRID_W, GRID_W)
        q = (q_ref[pl.ds(qs, nq), :].astype(F32) * (NA_HEAD_DIM ** -0.5 * LOG2E)).astype(BF16)
        kw = k_ref[pl.ds(ks, nk), :]
        vw = v_ref[pl.ds(ks, nk), :]
        outs = []
        for hh in range(2):
            keep = first if hh == 0 else jnp.logical_not(first)
            qh = jnp.where(keep, q, jnp.zeros_like(q))
            s = lax.dot_general(qh, kw, (((1,), (1,)), ((), ())), preferred_element_type=F32)
            s = s + bias_s[variant, hh]
            m = jnp.max(s, axis=-1, keepdims=True)
            p = jnp.exp2(s - m)
            l = jnp.sum(p, axis=-1, keepdims=True)
            o = jnp.dot(p.astype(BF16), vw, preferred_element_type=F32)
            outs.append(o / l)
        o_ref[pl.ds(qs, nq), :] = jnp.where(first, outs[0], outs[1]).astype(o_ref.dtype)
        return carry

    lax.fori_loop(0, ngroups, group_body, 0, unroll=4)


def natten_call(qkv, blocks):
    bsz, s, _ = qkv.shape
    rows = s // GRID_W
    assert rows % Q_GROUP == 0 and rows >= K_SLAB + Q_GROUP and K_SLAB % 2 == 0
    npairs = NA_HEADS // 2
    blk = lambda off: pl.BlockSpec((None, s, LANES), lambda hp, b: (b, 0, off + hp))
    return pl.pallas_call(
        functools.partial(_natten_kernel, rows=rows),
        grid=(npairs, bsz),
        in_specs=[blk(0), blk(npairs), blk(2 * npairs),
                  pl.BlockSpec((2,) + blocks.shape[1:], lambda hp, b: (hp, 0, 0, 0))],
        out_specs=pl.BlockSpec((None, s, LANES), lambda hp, b: (b, 0, hp)),
        out_shape=jax.ShapeDtypeStruct((bsz, s, ATTN_W), BF16),
        scratch_shapes=[pltpu.VMEM((3, 2, Q_GROUP * GRID_W, K_SLAB * GRID_W), F32)],
        compiler_params=_cparams("arbitrary", "arbitrary"),
        name="natten",
    )(qkv, qkv, qkv, blocks)


def build_bias_blocks(rpb):
    w = np.arange(GRID_W)[:, None]
    kc = np.arange(GRID_W)[None, :]
    c0 = np.clip(w - WIN_C // 2, 0, GRID_W - WIN_C)
    col_ok = (kc >= c0) & (kc < c0 + WIN_C)
    col_sel = np.zeros((GRID_W, GRID_W, 2 * WIN_C - 1), np.float32)
    col_sel[w, kc, np.clip(kc - w + WIN_C - 1, 0, 2 * WIN_C - 2)] = col_ok
    blocks = jnp.einsum("hrd,wkd->hrwk", rpb.astype(F32) * LOG2E, col_sel, precision=lax.Precision.HIGHEST)
    blocks = jnp.where(col_ok[None, None], blocks, NEG_BIG)
    return jnp.concatenate([blocks, jnp.full_like(blocks[:, :1], NEG_BIG)], axis=1)


def _silu(x):
    return x * jax.nn.sigmoid(x)


def _softplus(x):
    return jnp.maximum(x, 0.0) + jnp.log1p(jnp.exp(-jnp.abs(x)))


def _chunk_scan(x, reverse):
    width = x.shape[1]
    lane = lax.broadcasted_iota(jnp.int32, x.shape, 1) & (CHUNK - 1)
    sh = 1
    while sh < CHUNK:
        if reverse:
            x = x + jnp.where(lane < CHUNK - sh, pltpu.roll(x, width - sh, 1), 0.0)
        else:
            x = x + jnp.where(lane >= sh, pltpu.roll(x, sh, 1), 0.0)
        sh *= 2
    return x


def _expand_heads_row(rows):
    lane = lax.broadcasted_iota(jnp.int32, (1, LANES), 1)
    first = lane < SSD_HEAD_DIM
    return jnp.concatenate([jnp.where(first, rows[0], rows[1]), jnp.where(first, rows[2], rows[3])], axis=1)


def _ssd_kernel(xs_ref, bm_ref, cm_ref, z_ref, dtr_ref, cwx_ref, cwb_ref, cwc_ref, cbx_ref, cbb_ref, cbc_ref,
                pa_ref, pb_ref, dsk_ref, nw_ref, spread_ref, o_ref,
                pad_x, pad_b, pad_c, xbd_s, bc_s, cc_s, bt_s, cb_s, rows_s, csb_s, y_s, st_s,
                *, seq):
    nchunk = seq // CHUNK
    halo = 8
    nh = HEADS_PER_GROUP
    lane256 = lax.broadcasted_iota(jnp.int32, (CHUNK, GROUP_W), 1)
    head_masks = [(lane256 >= SSD_HEAD_DIM * r) & (lane256 < SSD_HEAD_DIM * (r + 1)) for r in range(nh)]

    a_rows = jnp.concatenate([pa_ref[...]] * nchunk, axis=1)
    bias_rows = jnp.concatenate([pb_ref[...]] * nchunk, axis=1)
    is_bwd = lax.broadcasted_iota(jnp.int32, (2 * nh, seq), 0) >= nh
    dt = _softplus(dtr_ref[...] + bias_rows)
    adt = dt * a_rows
    pre = _chunk_scan(adt, reverse=False)
    suf = _chunk_scan(adt, reverse=True)
    cs = jnp.where(is_bwd, suf, pre)
    total = pre + suf - adt
    rows_s[0] = (cs - jnp.log(dt)) * LOG2E
    rows_s[1] = jnp.exp(total)
    rows_s[2] = dt * jnp.exp(total - cs)
    cs2 = cs * LOG2E
    hi = cs2.astype(BF16).astype(F32)
    mid = (cs2 - hi).astype(BF16).astype(F32)
    rows_s[3] = hi
    rows_s[4] = mid
    rows_s[5] = ((cs2 - hi) - mid).astype(BF16).astype(F32)

    def pad_copy(src_ref, pad_ref):
        width = src_ref.shape[-1]
        pad_ref[pl.ds(0, halo), :] = jnp.zeros((halo, width), F32)
        pad_ref[pl.ds(halo + seq, halo), :] = jnp.zeros((halo, width), F32)
        pad_ref[pl.ds(halo, seq), :] = src_ref[...].astype(F32)

    def conv_chunk(pad_ref, w_ref, b_ref, t0):
        acc = jnp.broadcast_to(b_ref[...], (CHUNK, pad_ref.shape[-1]))
        for k in range(CONV_K):
            acc = acc + pad_ref[pl.ds(t0 + halo + k - CONV_K // 2, CHUNK), :] * w_ref[k:k + 1, :]
        return _silu(acc)

    pad_copy(bm_ref, pad_b)
    pad_copy(cm_ref, pad_c)
    pad_copy(xs_ref, pad_x)
    for c in range(nchunk):
        t0 = c * CHUNK
        vb = conv_chunk(pad_b, cwb_ref, cbb_ref, t0)
        vc = conv_chunk(pad_c, cwc_ref, cbc_ref, t0).astype(BF16)
        vx = conv_chunk(pad_x, cwx_ref, cbx_ref, t0)
        bc_s[pl.ds(t0, CHUNK), :] = vb.astype(BF16)
        bt_s[:, pl.ds(t0, CHUNK)] = vb.T.astype(BF16)
        cc_s[pl.ds(t0, CHUNK), :] = vc
        y_s[pl.ds(t0, CHUNK), :] = vx * dsk_ref[...]
        vxb = vx.astype(BF16)
        for r in range(nh):
            xbd_s[pl.ds((c * nh + r) * CHUNK, CHUNK), :] = jnp.where(head_masks[r], vxb, jnp.zeros_like(vxb))
        cb_s[pl.ds(t0, CHUNK), :] = lax.dot_general(vc, vb.astype(BF16), (((1,), (1,)), ((), ())),
                                                    preferred_element_type=F32)
        pieces = jnp.concatenate([rows_s[3, :, pl.ds(t0, CHUNK)], rows_s[4, :, pl.ds(t0, CHUNK)],
                                  rows_s[5, :, pl.ds(t0, CHUNK)], jnp.zeros((CHUNK - 6 * nh, CHUNK), F32)], axis=0)
        csb_s[pl.ds(t0, CHUNK), :] = jnp.dot(pieces.T.astype(BF16), spread_ref[...], preferred_element_type=F32)

    row_i = lax.broadcasted_iota(jnp.int32, (CHUNK, CHUNK), 0)
    col_i = lax.broadcasted_iota(jnp.int32, (CHUNK, CHUNK), 1)
    st_s[...] = jnp.zeros(st_s.shape, F32)

    def scan_step(direction, c):
        causal = (row_i >= col_i) if direction == 0 else (row_i <= col_i)
        t0 = pl.multiple_of(c * CHUNK, CHUNK)
        ck = cc_s[pl.ds(t0, CHUNK), :]
        cb = cb_s[pl.ds(t0, CHUNK), :]
        bt = bt_s[:, pl.ds(t0, CHUNK)]
        src = rows_s[0, :, pl.ds(t0, CHUNK)]
        decay = rows_s[1, :, pl.ds(t0, CHUNK)]
        w_state = rows_s[2, :, pl.ds(t0, CHUNK)].astype(BF16)
        j0 = direction * nh
        ms, bts, w_out = [], [], []
        for r in range(nh):
            cs_col = csb_s[pl.ds(t0, CHUNK), (j0 + r) * LANES:(j0 + r + 1) * LANES]
            seg = cs_col - src[j0 + r:j0 + r + 1, :]
            ms.append((cb * jnp.exp2(jnp.where(causal, seg, NEG_BIG))).astype(BF16))
            bts.append(bt * w_state[j0 + r:j0 + r + 1, :])
            w_out.append(jnp.exp2(cs_col))
        lhs = jnp.concatenate([jnp.concatenate(ms, axis=1), jnp.concatenate(bts, axis=1)], axis=0)
        xbd = xbd_s[pl.ds(pl.multiple_of(c * (nh * CHUNK), nh * CHUNK), nh * CHUNK), :]
        both = jnp.dot(lhs, xbd, preferred_element_type=F32)
        first = col_i < SSD_HEAD_DIM
        w_out_x = jnp.concatenate([jnp.where(first, w_out[0], w_out[1]), jnp.where(first, w_out[2], w_out[3])], axis=1)
        prev = st_s[direction]
        y = both[:CHUNK] + jnp.dot(ck, prev.astype(BF16), preferred_element_type=F32) * w_out_x
        decay_x = _expand_heads_row([decay[j0 + r:j0 + r + 1, :] for r in range(nh)])
        st_s[direction] = prev * decay_x + both[CHUNK:]
        y_s[pl.ds(t0, CHUNK), :] = y_s[pl.ds(t0, CHUNK), :] + y

    def scan_body(i, carry):
        scan_step(0, i)
        scan_step(1, nchunk - 1 - i)
        return carry

    lax.fori_loop(0, nchunk, scan_body, 0, unroll=4)

    def fin(c, carry):
        t0 = pl.multiple_of(c * CHUNK, CHUNK)
        y = y_s[pl.ds(t0, CHUNK), :] * _silu(z_ref[pl.ds(t0, CHUNK), :].astype(F32))
        y = y * lax.rsqrt(jnp.mean(y * y, axis=-1, keepdims=True) + RMS_EPS)
        o_ref[pl.ds(t0, CHUNK), :] = (y * nw_ref[...]).astype(o_ref.dtype)
        return carry

    lax.fori_loop(0, nchunk, fin, 0, unroll=4)


def ssd_call(pf, col0, dtr, cw, cb, pa, pb, dsk, nw):
    bsz, s, _ = pf.shape
    g_ = SSD_GROUPS
    nh = HEADS_PER_GROUP
    xo, bo = col0 // GROUP_W, (col0 + D_INNER) // D_STATE
    co, zo = (col0 + D_INNER + g_ * D_STATE) // D_STATE, (col0 + D_INNER + 2 * g_ * D_STATE) // GROUP_W
    cwx, cwb, cwc = cw[:, :D_INNER], cw[:, D_INNER:D_INNER + g_ * D_STATE], cw[:, D_INNER + g_ * D_STATE:]
    cbx, cbb, cbc = cb[:, :D_INNER], cb[:, D_INNER:D_INNER + g_ * D_STATE], cb[:, D_INNER + g_ * D_STATE:]
    wide = lambda off: pl.BlockSpec((None, s, GROUP_W), lambda b, g: (b, 0, off + g))
    narrow = lambda off: pl.BlockSpec((None, s, D_STATE), lambda b, g: (b, 0, off + g))
    pw = lambda rows, width: pl.BlockSpec((rows, width), lambda b, g: (0, g))
    spread = np.zeros((LANES, 2 * nh * LANES), np.float32)
    for p in range(3):
        for j in range(2 * nh):
            spread[2 * nh * p + j, j * LANES:(j + 1) * LANES] = 1.0
    spread = jnp.asarray(spread, BF16)
    return pl.pallas_call(
        functools.partial(_ssd_kernel, seq=s),
        grid=(bsz, g_),
        in_specs=[wide(xo), narrow(bo), narrow(co), wide(zo),
                  pl.BlockSpec((None, 2 * nh, s), lambda b, g: (b, g, 0)),
                  pw(CONV_K, GROUP_W), pw(CONV_K, D_STATE), pw(CONV_K, D_STATE),
                  pw(1, GROUP_W), pw(1, D_STATE), pw(1, D_STATE),
                  pl.BlockSpec((None, 2 * nh, LANES), lambda b, g: (g, 0, 0)),
                  pl.BlockSpec((None, 2 * nh, LANES), lambda b, g: (g, 0, 0)),
                  pw(1, GROUP_W), pw(1, GROUP_W),
                  pl.BlockSpec(spread.shape, lambda b, g: (0, 0))],
        out_specs=pl.BlockSpec((None, s, GROUP_W), lambda b, g: (b, 0, g)),
        out_shape=jax.ShapeDtypeStruct((bsz, s, D_INNER), BF16),
        scratch_shapes=[pltpu.VMEM((s + 16, GROUP_W), F32),
                        pltpu.VMEM((s + 16, D_STATE), F32),
                        pltpu.VMEM((s + 16, D_STATE), F32),
                        pltpu.VMEM((nh * s, GROUP_W), BF16),
                        pltpu.VMEM((s, D_STATE), BF16),
                        pltpu.VMEM((s, D_STATE), BF16),
                        pltpu.VMEM((D_STATE, s), BF16),
                        pltpu.VMEM((s, CHUNK), F32),
                        pltpu.VMEM((6, 2 * nh, s), F32),
                        pltpu.VMEM((s, 2 * nh * LANES), F32),
                        pltpu.VMEM((s, GROUP_W), F32),
                        pltpu.VMEM((2, D_STATE, GROUP_W), F32)],
        compiler_params=_cparams("parallel", "parallel"),
        name="ssd",
    )(pf, pf, pf, pf, dtr, cwx, cwb, cwc, cbx, cbb, cbc, pa, pb, dsk, nw, spread)


def _mix_ffn_kernel(att_ref, yn_ref, g0_ref, g1_ref, bg0_ref, bg1_ref, h_ref, wa_ref, ws_ref, wo_ref,
                    l1g_ref, l1b_ref, w1_ref, w2_ref, l2g_ref, l2b_ref, of_ref, ob_ref, *, nsplit):
    ya = jnp.dot(att_ref[...], wa_ref[...], preferred_element_type=F32)
    ys = jnp.dot(yn_ref[...], ws_ref[...], preferred_element_type=F32)
    merged = (jax.nn.sigmoid(g0_ref[...].astype(F32) + bg0_ref[...]) * ya
              + jax.nn.sigmoid(g1_ref[...].astype(F32) + bg1_ref[...]) * ys)
    mix = jnp.dot(merged.astype(BF16), wo_ref[...], preferred_element_type=F32)
    h1 = _ln_rows(DN_ALPHA * h_ref[...] + mix, l1g_ref[...], l1b_ref[...])
    hb = h1.astype(BF16)
    step = w1_ref.shape[1] // nsplit
    acc = DN_ALPHA * h1
    for j in range(nsplit):
        a = jnp.dot(hb, w1_ref[:, j * step:(j + 1) * step], preferred_element_type=F32)
        a = jnp.square(jnp.maximum(a, 0.0)).astype(BF16)
        acc = acc + jnp.dot(a, w2_ref[j * step:(j + 1) * step, :], preferred_element_type=F32)
    y = _ln_rows(acc, l2g_ref[...], l2b_ref[...])
    of_ref[...] = y
    ob_ref[...] = y.astype(BF16)


def mix_ffn_call(att, yn, pf, gate_col, b_gate, h, layer, wa, ws, wo, l1g, l1b, w1, w2, l2g, l2b, tm=512, nsplit=4):
    t, d = h.shape
    gate_off = gate_col // d
    row = lambda width: pl.BlockSpec((tm, width), lambda i: (i, 0))
    const = lambda a: pl.BlockSpec((None,) + a.shape[1:], lambda i: (layer, 0, 0), pipeline_mode=pl.Buffered(1))
    vec = pl.BlockSpec((1, d), lambda i: (0, 0))
    bg = b_gate.reshape(1, 2 * d)
    return pl.pallas_call(
        functools.partial(_mix_ffn_kernel, nsplit=nsplit),
        grid=(t // tm,),
        in_specs=[row(d), row(yn.shape[1]),
                  pl.BlockSpec((tm, d), lambda i: (i, gate_off)),
                  pl.BlockSpec((tm, d), lambda i: (i, gate_off + 1)),
                  pl.BlockSpec((1, d), lambda i: (0, 0)), pl.BlockSpec((1, d), lambda i: (0, 1)),
                  row(d), const(wa), const(ws), const(wo), vec, vec, const(w1), const(w2), vec, vec],
        out_specs=[row(d), row(d)],
        out_shape=[jax.ShapeDtypeStruct((t, d), F32), jax.ShapeDtypeStruct((t, d), BF16)],
        compiler_params=_cparams("parallel"),
        name="mix_ffn",
    )(att, yn, pf, pf, bg, bg, h, wa, ws, wo, l1g.reshape(1, d), l1b.reshape(1, d), w1, w2,
      l2g.reshape(1, d), l2b.reshape(1, d))


PROJ_TILE = 1024
_Z_COL = 3 * ATTN_W
_XBC_COL = _Z_COL + D_INNER
_DT_COL = _XBC_COL + D_INNER + 2 * SSD_GROUPS * D_STATE
_GATE_COL = _DT_COL + 2 * SSD_HEADS
PROJ_ROW_STARTS = ([c for c in range(0, _Z_COL, PROJ_TILE)] + [c for c in range(_XBC_COL, _DT_COL, PROJ_TILE)]
                   + [c for c in range(_Z_COL, _XBC_COL, PROJ_TILE)]
                   + [c for c in range(_GATE_COL, _GATE_COL + 2 * D_MODEL, PROJ_TILE)])


def _split_w_in(w):
    nl, d = w.shape[0], w.shape[1]
    w_t = w.transpose(0, 2, 1).astype(BF16)
    w_dt = w_t[:, _DT_COL:_GATE_COL, :]
    w_dt = w_dt.reshape(nl, 2, SSD_GROUPS, HEADS_PER_GROUP, d).transpose(0, 2, 1, 3, 4).reshape(nl, 2 * SSD_HEADS, d)
    w_dt = jnp.pad(w_dt, ((0, 0), (0, LANES - 2 * SSD_HEADS), (0, 0)))
    return w_t, w_dt


def _group_rows(p):
    p = p.astype(F32).reshape(2, SSD_GROUPS, HEADS_PER_GROUP).transpose(1, 0, 2).reshape(SSD_GROUPS, 2 * HEADS_PER_GROUP)
    return jnp.broadcast_to(p[:, :, None], (SSD_GROUPS, 2 * HEADS_PER_GROUP, LANES))


def _layer(h, hb, bsz, s, layer, w_all, w_dt, conv_w, conv_b, a_log, dt_bias, d_skip, ssd_norm_w, rpb,
           wa, ws, b_gate, wo, ln1_g, ln1_b, w1, w2, ln2_g, ln2_b):
    t = bsz * s
    ssd_col = 3 * ATTN_W
    gate_col = ssd_col + 2 * D_INNER + 2 * SSD_GROUPS * D_STATE
    pf = proj_call(hb, w_all, layer, PROJ_ROW_STARTS, BF16, 2048, PROJ_TILE, "proj")
    dtr = proj_t_call(hb, w_dt, layer, bsz, "proj_dt")

    att = natten_call(pf.reshape(bsz, s, -1), build_bias_blocks(rpb))

    pa = _group_rows(-jnp.exp(a_log.astype(F32)))
    pb = _group_rows(dt_bias)
    dsk = jnp.repeat(d_skip.astype(F32), SSD_HEAD_DIM).reshape(1, D_INNER)
    yn = ssd_call(pf.reshape(bsz, s, -1), ssd_col, dtr, conv_w.astype(F32), conv_b.astype(F32).reshape(1, -1),
                  pa, pb, dsk, ssd_norm_w.astype(F32).reshape(1, D_INNER))

    return mix_ffn_call(att.reshape(t, ATTN_W), yn.reshape(t, D_INNER), pf, gate_col, b_gate.astype(F32), h,
                        layer, wa, ws, wo, ln1_g, ln1_b, w1, w2, ln2_g, ln2_b)


def kernel(x, ln0_g, ln0_b, w_in, conv_w, conv_b, a_log, dt_bias, d_skip, ssd_norm_w, rpb, w_attn_br, w_ssd_br,
           b_gate, w_o, ln1_g, ln1_b, w_ff1, w_ff2, ln2_g, ln2_b):
    bsz, s, d = x.shape
    h, hb = layer_norm_call(x.reshape(bsz * s, d), ln0_g, ln0_b)
    w_all, w_dt = _split_w_in(w_in)
    wa, ws, wo = w_attn_br.astype(BF16), w_ssd_br.astype(BF16), w_o.astype(BF16)
    w1, w2 = w_ff1.astype(BF16), w_ff2.astype(BF16)
    for l in range(w_in.shape[0]):
        h, hb = _layer(h, hb, bsz, s, l, w_all, w_dt, conv_w[l], conv_b[l], a_log[l], dt_bias[l], d_skip[l],
                       ssd_norm_w[l], rpb[l], wa, ws, b_gate[l], wo, ln1_g[l], ln1_b[l], w1, w2, ln2_g[l], ln2_b[l])
    return h.reshape(bsz, s, d)
```

```python
import functools
import math

import jax
import jax.numpy as jnp
import numpy as np
from jax import lax
from jax.experimental import pallas as pl
from jax.experimental.pallas import tpu as pltpu

D_MODEL = 1024
DEPTH = 2
GRID_W = 64
NA_HEADS = 16
NA_HEAD_DIM = 64
ATTN_W = NA_HEADS * NA_HEAD_DIM
WIN_R = 8
WIN_C = 16
D_INNER = 2048
SSD_HEAD_DIM = 64
SSD_HEADS = D_INNER // SSD_HEAD_DIM
SSD_GROUPS = 8
HEADS_PER_GROUP = SSD_HEADS // SSD_GROUPS
D_STATE = 128
CONV_K = 5
CHUNK = 128
GROUP_W = D_INNER // SSD_GROUPS
D_FF = 4 * D_MODEL
DN_ALPHA = (2 * DEPTH) ** 0.25
LN_EPS = 1e-5
LOG2E = math.log2(math.e)
RMS_EPS = 1e-5

V7X_VMEM_BYTES = 64 * 1024 * 1024
VMEM_LIMIT = 56 * 1024 * 1024
LANES = 128
NEG_BIG = -1e30

F32 = jnp.float32
BF16 = jnp.bfloat16


def _cparams(*sem):
    return pltpu.CompilerParams(dimension_semantics=sem, vmem_limit_bytes=VMEM_LIMIT)


def _ln_rows(x, g, b):
    mu = jnp.mean(x, axis=-1, keepdims=True)
    xc = x - mu
    var = jnp.mean(xc * xc, axis=-1, keepdims=True)
    return xc * lax.rsqrt(var + LN_EPS) * g + b


def _ln_kernel(x_ref, g_ref, b_ref, of_ref, ob_ref):
    y = _ln_rows(x_ref[...], g_ref[...], b_ref[...])
    of_ref[...] = y
    ob_ref[...] = y.astype(BF16)


def layer_norm_call(x, g, b, tm=512):
    t, d = x.shape
    return pl.pallas_call(
        _ln_kernel,
        grid=(t // tm,),
        in_specs=[pl.BlockSpec((tm, d), lambda i: (i, 0)),
                  pl.BlockSpec((1, d), lambda i: (0, 0)),
                  pl.BlockSpec((1, d), lambda i: (0, 0))],
        out_specs=[pl.BlockSpec((tm, d), lambda i: (i, 0)),
                   pl.BlockSpec((tm, d), lambda i: (i, 0))],
        out_shape=[jax.ShapeDtypeStruct((t, d), F32), jax.ShapeDtypeStruct((t, d), BF16)],
        compiler_params=_cparams("parallel"),
        name="ln0",
    )(x, g.reshape(1, d), b.reshape(1, d))


def _proj_t_kernel(x_ref, w_ref, o_ref):
    o_ref[...] = lax.dot_general(w_ref[...], x_ref[...], (((1,), (1,)), ((), ())), preferred_element_type=F32)


def proj_t_call(x, w_t, layer, bsz, name):
    t, k = x.shape
    s = t // bsz
    r = w_t.shape[1]
    return pl.pallas_call(
        _proj_t_kernel,
        grid=(bsz,),
        in_specs=[pl.BlockSpec((s, k), lambda b: (b, 0)),
                  pl.BlockSpec((None, r, k), lambda b: (layer, 0, 0))],
        out_specs=pl.BlockSpec((None, r, s), lambda b: (b, 0, 0)),
        out_shape=jax.ShapeDtypeStruct((bsz, r, s), F32),
        compiler_params=_cparams("parallel"),
        name=name,
    )(x, w_t)


def _proj_kernel(x_ref, w_ref, o_ref, *, row_starts):
    tn = o_ref.shape[1]
    j = pl.program_id(1)
    row = row_starts[0]
    for idx in range(1, len(row_starts)):
        row = jnp.where(j >= idx, row_starts[idx], row)
    w = w_ref[pl.ds(pl.multiple_of(row, 64), tn), :]
    o_ref[...] = lax.dot_general(x_ref[...], w, (((1,), (1,)), ((), ())),
                                 preferred_element_type=F32).astype(o_ref.dtype)


def proj_call(x, w_t, layer, row_starts, out_dtype, tm, tn, name):
    t, k = x.shape
    n_rows = w_t.shape[1]
    n = tn * len(row_starts)
    return pl.pallas_call(
        functools.partial(_proj_kernel, row_starts=tuple(row_starts)),
        grid=(t // tm, len(row_starts)),
        in_specs=[pl.BlockSpec((tm, k), lambda i, j: (i, 0)),
                  pl.BlockSpec((None, n_rows, k), lambda i, j: (layer, 0, 0), pipeline_mode=pl.Buffered(1))],
        out_specs=pl.BlockSpec((tm, tn), lambda i, j: (i, j)),
        out_shape=jax.ShapeDtypeStruct((t, n), out_dtype),
        compiler_params=_cparams("parallel", "arbitrary"),
        name=name,
    )(x, w_t)


Q_GROUP = 4
K_SLAB = Q_GROUP + WIN_R


def _slab_start(j, rows):
    return jnp.clip(Q_GROUP * j - WIN_R // 2, 0, rows - K_SLAB)


def _group_patterns(rows):
    ngroups = rows // Q_GROUP
    outside = 2 * WIN_R - 1
    pats = []
    for j in range(ngroups):
        ws = min(max(Q_GROUP * j - WIN_R // 2, 0), rows - K_SLAB)
        pat = np.full((Q_GROUP, K_SLAB), outside, np.int64)
        for a in range(Q_GROUP):
            r = Q_GROUP * j + a
            r0 = min(max(r - WIN_R // 2, 0), rows - WIN_R)
            for i in range(K_SLAB):
                if r0 <= ws + i < r0 + WIN_R:
                    pat[a, i] = ws + i - r + WIN_R - 1
        pats.append(pat)
    assert all(np.array_equal(pats[1], p) for p in pats[1:-1])
    return [pats[0].tolist(), pats[1].tolist(), pats[-1].tolist()]


def _natten_kernel(q_ref, k_ref, v_ref, blk_ref, o_ref, bias_s, *, rows):
    ngroups = rows // Q_GROUP
    nq = Q_GROUP * GRID_W
    nk = K_SLAB * GRID_W
    lane = lax.broadcasted_iota(jnp.int32, (nq, LANES), 1)
    first = lane < NA_HEAD_DIM

    @pl.when(pl.program_id(1) == 0)
    def _():
        for v, pat in enumerate(_group_patterns(rows)):
            for hh in range(2):
                for a in range(Q_GROUP):
                    for i in range(0, K_SLAB, 2):
                        pair = jnp.concatenate([blk_ref[hh, pat[a][i]], blk_ref[hh, pat[a][i + 1]]], axis=1)
                        bias_s[v, hh, a * GRID_W:(a + 1) * GRID_W, i * GRID_W:(i + 2) * GRID_W] = pair

    def group_body(j, carry):
        variant = jnp.minimum(j, 1) + jnp.maximum(j - (ngroups - 2), 0)
        qs = pl.multiple_of(j * nq, nq)
        ks = pl.multiple_of(_slab_start(j, rows) * GRID_W, GRID_W)
        q = (q_ref[pl.ds(qs, nq), :].astype(F32) * (NA_HEAD_DIM ** -0.5 * LOG2E)).astype(BF16)
        kw = k_ref[pl.ds(ks, nk), :]
        vw = v_ref[pl.ds(ks, nk), :]
        outs = []
        for hh in range(2):
            keep = first if hh == 0 else jnp.logical_not(first)
            qh = jnp.where(keep, q, jnp.zeros_like(q))
            s = lax.dot_general(qh, kw, (((1,), (1,)), ((), ())), preferred_element_type=F32)
            s = s + bias_s[variant, hh]
            m = jnp.max(s, axis=-1, keepdims=True)
            p = jnp.exp2(s - m)
            l = jnp.sum(p, axis=-1, keepdims=True)
            o = jnp.dot(p.astype(BF16), vw, preferred_element_type=F32)
            outs.append(o / l)
        o_ref[pl.ds(qs, nq), :] = jnp.where(first, outs[0], outs[1]).astype(o_ref.dtype)
        return carry

    lax.fori_loop(0, ngroups, group_body, 0, unroll=8)


def natten_call(qkv, blocks):
    bsz, s, _ = qkv.shape
    rows = s // GRID_W
    assert rows % Q_GROUP == 0 and rows >= K_SLAB + Q_GROUP and K_SLAB % 2 == 0
    npairs = NA_HEADS // 2
    blk = lambda off: pl.BlockSpec((None, s, LANES), lambda hp, b: (b, 0, off + hp))
    return pl.pallas_call(
        functools.partial(_natten_kernel, rows=rows),
        grid=(npairs, bsz),
        in_specs=[blk(0), blk(npairs), blk(2 * npairs),
                  pl.BlockSpec((2,) + blocks.shape[1:], lambda hp, b: (hp, 0, 0, 0))],
        out_specs=pl.BlockSpec((None, s, LANES), lambda hp, b: (b, 0, hp)),
        out_shape=jax.ShapeDtypeStruct((bsz, s, ATTN_W), BF16),
        scratch_shapes=[pltpu.VMEM((3, 2, Q_GROUP * GRID_W, K_SLAB * GRID_W), F32)],
        compiler_params=_cparams("arbitrary", "arbitrary"),
        name="natten",
    )(qkv, qkv, qkv, blocks)


def build_bias_blocks(rpb):
    w = np.arange(GRID_W)[:, None]
    kc = np.arange(GRID_W)[None, :]
    c0 = np.clip(w - WIN_C // 2, 0, GRID_W - WIN_C)
    col_ok = (kc >= c0) & (kc < c0 + WIN_C)
    col_sel = np.zeros((GRID_W, GRID_W, 2 * WIN_C - 1), np.float32)
    col_sel[w, kc, np.clip(kc - w + WIN_C - 1, 0, 2 * WIN_C - 2)] = col_ok
    blocks = jnp.einsum("hrd,wkd->hrwk", rpb.astype(F32) * LOG2E, col_sel, precision=lax.Precision.HIGHEST)
    blocks = jnp.where(col_ok[None, None], blocks, NEG_BIG)
    return jnp.concatenate([blocks, jnp.full_like(blocks[:, :1], NEG_BIG)], axis=1)


def _silu(x):
    return x * jax.nn.sigmoid(x)


def _softplus(x):
    return jnp.maximum(x, 0.0) + jnp.log1p(jnp.exp(-jnp.abs(x)))


def _chunk_scan(x, reverse):
    width = x.shape[1]
    lane = lax.broadcasted_iota(jnp.int32, x.shape, 1) & (CHUNK - 1)
    sh = 1
    while sh < CHUNK:
        if reverse:
            x = x + jnp.where(lane < CHUNK - sh, pltpu.roll(x, width - sh, 1), 0.0)
        else:
            x = x + jnp.where(lane >= sh, pltpu.roll(x, sh, 1), 0.0)
        sh *= 2
    return x


def _expand_heads_row(rows):
    lane = lax.broadcasted_iota(jnp.int32, (1, LANES), 1)
    first = lane < SSD_HEAD_DIM
    return jnp.concatenate([jnp.where(first, rows[0], rows[1]), jnp.where(first, rows[2], rows[3])], axis=1)


def _ssd_kernel(xs_ref, bm_ref, cm_ref, z_ref, dtr_ref, cwx_ref, cwb_ref, cwc_ref, cbx_ref, cbb_ref, cbc_ref,
                pa_ref, pb_ref, dsk_ref, nw_ref, spread_ref, o_ref,
                pad_x, pad_b, pad_c, xbd_s, bc_s, cc_s, bt_s, cb_s, rows_s, csb_s, y_s, st_s,
                *, seq):
    nchunk = seq // CHUNK
    halo = 8
    nh = HEADS_PER_GROUP
    lane256 = lax.broadcasted_iota(jnp.int32, (CHUNK, GROUP_W), 1)
    head_masks = [(lane256 >= SSD_HEAD_DIM * r) & (lane256 < SSD_HEAD_DIM * (r + 1)) for r in range(nh)]

    a_rows = jnp.concatenate([pa_ref[...]] * nchunk, axis=1)
    bias_rows = jnp.concatenate([pb_ref[...]] * nchunk, axis=1)
    is_bwd = lax.broadcasted_iota(jnp.int32, (2 * nh, seq), 0) >= nh
    dt = _softplus(dtr_ref[...] + bias_rows)
    adt = dt * a_rows
    pre = _chunk_scan(adt, reverse=False)
    suf = _chunk_scan(adt, reverse=True)
    cs = jnp.where(is_bwd, suf, pre)
    total = pre + suf - adt
    rows_s[0] = (cs - jnp.log(dt)) * LOG2E
    rows_s[1] = jnp.exp(total)
    rows_s[2] = dt * jnp.exp(total - cs)
    cs2 = cs * LOG2E
    hi = cs2.astype(BF16).astype(F32)
    mid = (cs2 - hi).astype(BF16).astype(F32)
    rows_s[3] = hi
    rows_s[4] = mid
    rows_s[5] = ((cs2 - hi) - mid).astype(BF16).astype(F32)

    def pad_copy(src_ref, pad_ref):
        width = src_ref.shape[-1]
        pad_ref[pl.ds(0, halo), :] = jnp.zeros((halo, width), F32)
        pad_ref[pl.ds(halo + seq, halo), :] = jnp.zeros((halo, width), F32)
        pad_ref[pl.ds(halo, seq), :] = src_ref[...].astype(F32)

    def conv_chunk(pad_ref, w_ref, b_ref, t0):
        acc = jnp.broadcast_to(b_ref[...], (CHUNK, pad_ref.shape[-1]))
        for k in range(CONV_K):
            acc = acc + pad_ref[pl.ds(t0 + halo + k - CONV_K // 2, CHUNK), :] * w_ref[k:k + 1, :]
        return _silu(acc)

    pad_copy(bm_ref, pad_b)
    pad_copy(cm_ref, pad_c)
    pad_copy(xs_ref, pad_x)
    for c in range(nchunk):
        t0 = c * CHUNK
        vb = conv_chunk(pad_b, cwb_ref, cbb_ref, t0)
        vc = conv_chunk(pad_c, cwc_ref, cbc_ref, t0).astype(BF16)
        vx = conv_chunk(pad_x, cwx_ref, cbx_ref, t0)
        bc_s[pl.ds(t0, CHUNK), :] = vb.astype(BF16)
        bt_s[:, pl.ds(t0, CHUNK)] = vb.T.astype(BF16)
        cc_s[pl.ds(t0, CHUNK), :] = vc
        y_s[pl.ds(t0, CHUNK), :] = vx * dsk_ref[...]
        vxb = vx.astype(BF16)
        for r in range(nh):
            xbd_s[pl.ds((c * nh + r) * CHUNK, CHUNK), :] = jnp.where(head_masks[r], vxb, jnp.zeros_like(vxb))
        cb_s[pl.ds(t0, CHUNK), :] = lax.dot_general(vc, vb.astype(BF16), (((1,), (1,)), ((), ())),
                                                    preferred_element_type=F32)
        pieces = jnp.concatenate([rows_s[3, :, pl.ds(t0, CHUNK)], rows_s[4, :, pl.ds(t0, CHUNK)],
                                  rows_s[5, :, pl.ds(t0, CHUNK)], jnp.zeros((CHUNK - 6 * nh, CHUNK), F32)], axis=0)
        csb_s[pl.ds(t0, CHUNK), :] = jnp.dot(pieces.T.astype(BF16), spread_ref[...], preferred_element_type=F32)

    row_i = lax.broadcasted_iota(jnp.int32, (CHUNK, CHUNK), 0)
    col_i = lax.broadcasted_iota(jnp.int32, (CHUNK, CHUNK), 1)
    st_s[...] = jnp.zeros(st_s.shape, F32)

    def scan_step(direction, c):
        causal = (row_i >= col_i) if direction == 0 else (row_i <= col_i)
        t0 = pl.multiple_of(c * CHUNK, CHUNK)
        ck = cc_s[pl.ds(t0, CHUNK), :]
        cb = cb_s[pl.ds(t0, CHUNK), :]
        bt = bt_s[:, pl.ds(t0, CHUNK)]
        src = rows_s[0, :, pl.ds(t0, CHUNK)]
        decay = rows_s[1, :, pl.ds(t0, CHUNK)]
        w_state = rows_s[2, :, pl.ds(t0, CHUNK)].astype(BF16)
        j0 = direction * nh
        ms, bts, w_out = [], [], []
        for r in range(nh):
            cs_col = csb_s[pl.ds(t0, CHUNK), (j0 + r) * LANES:(j0 + r + 1) * LANES]
            seg = cs_col - src[j0 + r:j0 + r + 1, :]
            ms.append((cb * jnp.exp2(jnp.where(causal, seg, NEG_BIG))).astype(BF16))
            bts.append(bt * w_state[j0 + r:j0 + r + 1, :])
            w_out.append(jnp.exp2(cs_col))
        lhs = jnp.concatenate([jnp.concatenate(ms, axis=1), jnp.concatenate(bts, axis=1)], axis=0)
        xbd = xbd_s[pl.ds(pl.multiple_of(c * (nh * CHUNK), nh * CHUNK), nh * CHUNK), :]
        both = jnp.dot(lhs, xbd, preferred_element_type=F32)
        first = col_i < SSD_HEAD_DIM
        w_out_x = jnp.concatenate([jnp.where(first, w_out[0], w_out[1]), jnp.where(first, w_out[2], w_out[3])], axis=1)
        prev = st_s[direction]
        y = both[:CHUNK] + jnp.dot(ck, prev.astype(BF16), preferred_element_type=F32) * w_out_x
        decay_x = _expand_heads_row([decay[j0 + r:j0 + r + 1, :] for r in range(nh)])
        st_s[direction] = prev * decay_x + both[CHUNK:]
        y_s[pl.ds(t0, CHUNK), :] = y_s[pl.ds(t0, CHUNK), :] + y

    def scan_body(i, carry):
        scan_step(0, i)
        scan_step(1, nchunk - 1 - i)
        return carry

    lax.fori_loop(0, nchunk, scan_body, 0, unroll=8)

    def fin(c, carry):
        t0 = pl.multiple_of(c * CHUNK, CHUNK)
        y = y_s[pl.ds(t0, CHUNK), :] * _silu(z_ref[pl.ds(t0, CHUNK), :].astype(F32))
        y = y * lax.rsqrt(jnp.mean(y * y, axis=-1, keepdims=True) + RMS_EPS)
        o_ref[pl.ds(t0, CHUNK), :] = (y * nw_ref[...]).astype(o_ref.dtype)
        return carry

    lax.fori_loop(0, nchunk, fin, 0, unroll=8)


def ssd_call(pf, col0, dtr, cw, cb, pa, pb, dsk, nw):
    bsz, s, _ = pf.shape
    g_ = SSD_GROUPS
    nh = HEADS_PER_GROUP
    xo, bo = col0 // GROUP_W, (col0 + D_INNER) // D_STATE
    co, zo = (col0 + D_INNER + g_ * D_STATE) // D_STATE, (col0 + D_INNER + 2 * g_ * D_STATE) // GROUP_W
    cwx, cwb, cwc = cw[:, :D_INNER], cw[:, D_INNER:D_INNER + g_ * D_STATE], cw[:, D_INNER + g_ * D_STATE:]
    cbx, cbb, cbc = cb[:, :D_INNER], cb[:, D_INNER:D_INNER + g_ * D_STATE], cb[:, D_INNER + g_ * D_STATE:]
    wide = lambda off: pl.BlockSpec((None, s, GROUP_W), lambda b, g: (b, 0, off + g))
    narrow = lambda off: pl.BlockSpec((None, s, D_STATE), lambda b, g: (b, 0, off + g))
    pw = lambda rows, width: pl.BlockSpec((rows, width), lambda b, g: (0, g))
    spread = np.zeros((LANES, 2 * nh * LANES), np.float32)
    for p in range(3):
        for j in range(2 * nh):
            spread[2 * nh * p + j, j * LANES:(j + 1) * LANES] = 1.0
    spread = jnp.asarray(spread, BF16)
    return pl.pallas_call(
        functools.partial(_ssd_kernel, seq=s),
        grid=(bsz, g_),
        in_specs=[wide(xo), narrow(bo), narrow(co), wide(zo),
                  pl.BlockSpec((None, 2 * nh, s), lambda b, g: (b, g, 0)),
                  pw(CONV_K, GROUP_W), pw(CONV_K, D_STATE), pw(CONV_K, D_STATE),
                  pw(1, GROUP_W), pw(1, D_STATE), pw(1, D_STATE),
                  pl.BlockSpec((None, 2 * nh, LANES), lambda b, g: (g, 0, 0)),
                  pl.BlockSpec((None, 2 * nh, LANES), lambda b, g: (g, 0, 0)),
                  pw(1, GROUP_W), pw(1, GROUP_W),
                  pl.BlockSpec(spread.shape, lambda b, g: (0, 0))],
        out_specs=pl.BlockSpec((None, s, GROUP_W), lambda b, g: (b, 0, g)),
        out_shape=jax.ShapeDtypeStruct((bsz, s, D_INNER), BF16),
        scratch_shapes=[pltpu.VMEM((s + 16, GROUP_W), F32),
                        pltpu.VMEM((s + 16, D_STATE), F32),
                        pltpu.VMEM((s + 16, D_STATE), F32),
                        pltpu.VMEM((nh * s, GROUP_W), BF16),
                        pltpu.VMEM((s, D_STATE), BF16),
                        pltpu.VMEM((s, D_STATE), BF16),
                        pltpu.VMEM((D_STATE, s), BF16),
                        pltpu.VMEM((s, CHUNK), F32),
                        pltpu.VMEM((6, 2 * nh, s), F32),
                        pltpu.VMEM((s, 2 * nh * LANES), F32),
                        pltpu.VMEM((s, GROUP_W), F32),
                        pltpu.VMEM((2, D_STATE, GROUP_W), F32)],
        compiler_params=_cparams("parallel", "parallel"),
        name="ssd",
    )(pf, pf, pf, pf, dtr, cwx, cwb, cwc, cbx, cbb, cbc, pa, pb, dsk, nw, spread)


def _mix_ffn_kernel(att_ref, yn_ref, g0_ref, g1_ref, bg0_ref, bg1_ref, h_ref, wa_ref, ws_ref, wo_ref,
                    l1g_ref, l1b_ref, w1_ref, w2_ref, l2g_ref, l2b_ref, of_ref, ob_ref, *, nsplit):
    ya = jnp.dot(att_ref[...], wa_ref[...], preferred_element_type=F32)
    ys = jnp.dot(yn_ref[...], ws_ref[...], preferred_element_type=F32)
    merged = (jax.nn.sigmoid(g0_ref[...].astype(F32) + bg0_ref[...]) * ya
              + jax.nn.sigmoid(g1_ref[...].astype(F32) + bg1_ref[...]) * ys)
    mix = jnp.dot(merged.astype(BF16), wo_ref[...], preferred_element_type=F32)
    h1 = _ln_rows(DN_ALPHA * h_ref[...] + mix, l1g_ref[...], l1b_ref[...])
    hb = h1.astype(BF16)
    step = w1_ref.shape[1] // nsplit
    acc = DN_ALPHA * h1
    for j in range(nsplit):
        a = jnp.dot(hb, w1_ref[:, j * step:(j + 1) * step], preferred_element_type=F32)
        a = jnp.square(jnp.maximum(a, 0.0)).astype(BF16)
        acc = acc + jnp.dot(a, w2_ref[j * step:(j + 1) * step, :], preferred_element_type=F32)
    y = _ln_rows(acc, l2g_ref[...], l2b_ref[...])
    of_ref[...] = y
    ob_ref[...] = y.astype(BF16)


def mix_ffn_call(att, yn, pf, gate_col, b_gate, h, layer, wa, ws, wo, l1g, l1b, w1, w2, l2g, l2b, tm=512, nsplit=4):
    t, d = h.shape
    gate_off = gate_col // d
    row = lambda width: pl.BlockSpec((tm, width), lambda i: (i, 0))
    const = lambda a: pl.BlockSpec((None,) + a.shape[1:], lambda i: (layer, 0, 0), pipeline_mode=pl.Buffered(1))
    vec = pl.BlockSpec((1, d), lambda i: (0, 0))
    bg = b_gate.reshape(1, 2 * d)
    return pl.pallas_call(
        functools.partial(_mix_ffn_kernel, nsplit=nsplit),
        grid=(t // tm,),
        in_specs=[row(d), row(yn.shape[1]),
                  pl.BlockSpec((tm, d), lambda i: (i, gate_off)),
                  pl.BlockSpec((tm, d), lambda i: (i, gate_off + 1)),
                  pl.BlockSpec((1, d), lambda i: (0, 0)), pl.BlockSpec((1, d), lambda i: (0, 1)),
                  row(d), const(wa), const(ws), const(wo), vec, vec, const(w1), const(w2), vec, vec],
        out_specs=[row(d), row(d)],
        out_shape=[jax.ShapeDtypeStruct((t, d), F32), jax.ShapeDtypeStruct((t, d), BF16)],
        compiler_params=_cparams("parallel"),
        name="mix_ffn",
    )(att, yn, pf, pf, bg, bg, h, wa, ws, wo, l1g.reshape(1, d), l1b.reshape(1, d), w1, w2,
      l2g.reshape(1, d), l2b.reshape(1, d))


PROJ_TILE = 1024
_Z_COL = 3 * ATTN_W
_XBC_COL = _Z_COL + D_INNER
_DT_COL = _XBC_COL + D_INNER + 2 * SSD_GROUPS * D_STATE
_GATE_COL = _DT_COL + 2 * SSD_HEADS
PROJ_ROW_STARTS = ([c for c in range(0, _Z_COL, PROJ_TILE)] + [c for c in range(_XBC_COL, _DT_COL, PROJ_TILE)]
                   + [c for c in range(_Z_COL, _XBC_COL, PROJ_TILE)]
                   + [c for c in range(_GATE_COL, _GATE_COL + 2 * D_MODEL, PROJ_TILE)])


def _split_w_in(w):
    nl, d = w.shape[0], w.shape[1]
    w_t = w.transpose(0, 2, 1).astype(BF16)
    w_dt = w_t[:, _DT_COL:_GATE_COL, :]
    w_dt = w_dt.reshape(nl, 2, SSD_GROUPS, HEADS_PER_GROUP, d).transpose(0, 2, 1, 3, 4).reshape(nl, 2 * SSD_HEADS, d)
    w_dt = jnp.pad(w_dt, ((0, 0), (0, LANES - 2 * SSD_HEADS), (0, 0)))
    return w_t, w_dt


def _group_rows(p):
    p = p.astype(F32).reshape(2, SSD_GROUPS, HEADS_PER_GROUP).transpose(1, 0, 2).reshape(SSD_GROUPS, 2 * HEADS_PER_GROUP)
    return jnp.broadcast_to(p[:, :, None], (SSD_GROUPS, 2 * HEADS_PER_GROUP, LANES))


def _layer(h, hb, bsz, s, layer, w_all, w_dt, conv_w, conv_b, a_log, dt_bias, d_skip, ssd_norm_w, rpb,
           wa, ws, b_gate, wo, ln1_g, ln1_b, w1, w2, ln2_g, ln2_b):
    t = bsz * s
    ssd_col = 3 * ATTN_W
    gate_col = ssd_col + 2 * D_INNER + 2 * SSD_GROUPS * D_STATE
    pf = proj_call(hb, w_all, layer, PROJ_ROW_STARTS, BF16, 2048, PROJ_TILE, "proj")
    dtr = proj_t_call(hb, w_dt, layer, bsz, "proj_dt")

    att = natten_call(pf.reshape(bsz, s, -1), build_bias_blocks(rpb))

    pa = _group_rows(-jnp.exp(a_log.astype(F32)))
    pb = _group_rows(dt_bias)
    dsk = jnp.repeat(d_skip.astype(F32), SSD_HEAD_DIM).reshape(1, D_INNER)
    yn = ssd_call(pf.reshape(bsz, s, -1), ssd_col, dtr, conv_w.astype(F32), conv_b.astype(F32).reshape(1, -1),
                  pa, pb, dsk, ssd_norm_w.astype(F32).reshape(1, D_INNER))

    return mix_ffn_call(att.reshape(t, ATTN_W), yn.reshape(t, D_INNER), pf, gate_col, b_gate.astype(F32), h,
                        layer, wa, ws, wo, ln1_g, ln1_b, w1, w2, ln2_g, ln2_b)


def kernel(x, ln0_g, ln0_b, w_in, conv_w, conv_b, a_log, dt_bias, d_skip, ssd_norm_w, rpb, w_attn_br, w_ssd_br,
           b_gate, w_o, ln1_g, ln1_b, w_ff1, w_ff2, ln2_g, ln2_b):
    bsz, s, d = x.shape
    h, hb = layer_norm_call(x.reshape(bsz * s, d), ln0_g, ln0_b)
    w_all, w_dt = _split_w_in(w_in)
    wa, ws, wo = w_attn_br.astype(BF16), w_ssd_br.astype(BF16), w_o.astype(BF16)
    w1, w2 = w_ff1.astype(BF16), w_ff2.astype(BF16)
    for l in range(w_in.shape[0]):
        h, hb = _layer(h, hb, bsz, s, l, w_all, w_dt, conv_w[l], conv_b[l], a_log[l], dt_bias[l], d_skip[l],
                       ssd_norm_w[l], rpb[l], wa, ws, b_gate[l], wo, ln1_g[l], ln1_b[l], w1, w2, ln2_g[l], ln2_b[l])
    return h.reshape(bsz, s, d)
```
